```python
import math
import jax, jax.numpy as jnp
from jax import lax
import numpy as np

D_MODEL = 1024
BATCH = 2
SEQ = 8192
DEPTH = 2
DEC_BATCH = 128
DEC_SEQ = 4
PAST_LEN = 8192
PAGE_SIZE = 128

N_EVEN = (DEPTH + 1) // 2
N_ODD = DEPTH // 2
HALF = D_MODEL // 2
S5_WIDTH = HALF
S5_GROUP_CH = 16
S5_GROUPS = S5_WIDTH // S5_GROUP_CH
S5_STATE = 64
HG_WIDTH = HALF
HG_HEADS = 4
HG_HEAD_DIM = HG_WIDTH // HG_HEADS
SWA_HEADS = 8
SWA_KV_HEADS = 2
SWA_HEAD_DIM = HALF // SWA_HEADS
SWA_REP = SWA_HEADS // SWA_KV_HEADS
WINDOW = 128
W_BUF = min(WINDOW, PAST_LEN)
GLA_HEADS = 4
GLA_KEY_WIDTH = HALF // 2
GLA_VAL_WIDTH = HALF
GLA_DK = GLA_KEY_WIDTH // GLA_HEADS
GLA_DV = GLA_VAL_WIDTH // GLA_HEADS
GLA_RANK = 16
GLA_TAU = 16.0
LIN_CHUNK = 64
PEER_HEADS = 8
PEER_TOPK = 16
N_KEYS = 128
N_EXPERTS = N_KEYS * N_KEYS
PEER_QDIM = 256
PEER_QHALF = PEER_QDIM // 2
PEER_BLOCK = 128
EVEN_SIZES = [S5_WIDTH, HG_WIDTH, HG_WIDTH, HG_WIDTH, HG_WIDTH]
EVEN_IN = sum(EVEN_SIZES)
EVEN_OUT = S5_WIDTH + HG_WIDTH
ODD_SIZES = [SWA_HEADS * SWA_HEAD_DIM, SWA_KV_HEADS * SWA_HEAD_DIM, SWA_KV_HEADS * SWA_HEAD_DIM,
             GLA_KEY_WIDTH, GLA_KEY_WIDTH, GLA_VAL_WIDTH, GLA_RANK, GLA_VAL_WIDTH]
ODD_IN = sum(ODD_SIZES)
ODD_OUT = SWA_HEADS * SWA_HEAD_DIM + GLA_VAL_WIDTH
EPS = 1e-6

kernel_name = 'hybrid_s5_hgrn2_swa_gla_peer_step'

F32 = jnp.float32


def rmsnorm(x, g):
    xf = x.astype(F32)
    return xf * lax.rsqrt(jnp.mean(xf * xf, axis=-1, keepdims=True) + EPS) * g.astype(F32)


def split_cols(z, sizes):
    return jnp.split(z, [int(i) for i in np.cumsum(sizes)[:-1]], axis=-1)


def chunked_gla(q, k, v, log_f, s0):
    B, T, H, dk = q.shape
    dv = v.shape[-1]
    C = LIN_CHUNK if T % LIN_CHUNK == 0 else T
    n = T // C

    def to_chunks(a):
        return a.astype(F32).reshape(B, n, C, H, a.shape[-1]).transpose(1, 0, 2, 3, 4)

    qc, kc, vc, gc = to_chunks(q), to_chunks(k), to_chunks(v), to_chunks(log_f)
    causal = jnp.tril(jnp.ones((C, C), dtype=bool))[None, :, :, None, None]

    def step(S, inp):
        qb, kb, vb, gb = inp
        b = jnp.cumsum(gb, axis=1)
        inter = jnp.einsum('bchk,bhkv->bchv', qb * jnp.exp(b), S)
        diff = b[:, :, None] - b[:, None, :]
        decay = jnp.exp(jnp.where(causal, diff, -jnp.inf))
        scores = jnp.einsum('bthk,btshk,bshk->bths', qb, decay, kb)
        intra = jnp.einsum('bths,bshv->bthv', scores, vb)
        last = b[:, -1]
        S_new = jnp.exp(last)[..., None] * S + jnp.einsum(
            'bshk,bshv->bhkv', kb * jnp.exp(last[:, None] - b), vb)
        return S_new, inter + intra

    S_last, out = lax.scan(step, s0.astype(F32), (qc, kc, vc, gc))
    return out.transpose(1, 0, 2, 3, 4).reshape(B, T, H, dv), S_last


def s5_mixer(u, lam_re, lam_im, log_dt, b_re, b_im, c_re, c_im, d_skip, glu_w, glu_b, h0_re, h0_im):
    B, T, _ = u.shape
    uf = u.astype(F32).reshape(B, T, S5_GROUPS, S5_GROUP_CH)
    lam = lax.complex(lam_re.astype(F32), lam_im.astype(F32))
    dt = jnp.exp(log_dt.astype(F32))[:, None]
    lam_bar = jnp.exp(lam * dt)
    b_bar = ((lam_bar - 1.0) / lam)[..., None] * lax.complex(b_re.astype(F32), b_im.astype(F32))
    bu = jnp.einsum('btgc,gpc->btgp', uf.astype(jnp.complex64), b_bar)
    h0 = lax.complex(h0_re.astype(F32), h0_im.astype(F32))
    bu = bu.at[:, 0].add(lam_bar * h0)
    a = jnp.broadcast_to(lam_bar, bu.shape)

    def combine(left, right):
        a1, b1 = left
        a2, b2 = right
        return a1 * a2, a2 * b1 + b2

    _, h = lax.associative_scan(combine, (a, bu), axis=1)
    c_mat = lax.complex(c_re.astype(F32), c_im.astype(F32))
    y = jnp.real(jnp.einsum('btgp,gcp->btgc', h, c_mat)) + d_skip.astype(F32).reshape(S5_GROUPS, S5_GROUP_CH) * uf
    g = jax.nn.gelu(y.reshape(B, T, S5_WIDTH))
    out = g * jax.nn.sigmoid(g @ glu_w.astype(F32) + glu_b.astype(F32))
    h_last = h[:, -1]
    return out, jnp.real(h_last), jnp.imag(h_last)


def hgrn2_mixer(hq, hf, hi, hgate, lb, norm_g, s0):
    B, T, _ = hq.shape
    log_f = jnp.logaddexp(jnp.log(lb), jnp.log1p(-lb) + jax.nn.log_sigmoid(hf))
    k = (1.0 - lb) * jax.nn.sigmoid(-hf)

    def heads(a):
        return a.reshape(B, T, HG_HEADS, HG_HEAD_DIM)

    o, s_last = chunked_gla(heads(hq), heads(k), heads(hi), heads(log_f), s0)
    o = rmsnorm(o, norm_g.reshape(HG_HEADS, HG_HEAD_DIM)) * jax.nn.sigmoid(heads(hgate))
    return o.reshape(B, T, HG_WIDTH), s_last


def band_mask(lp, lq, prev_ok):
    qpos = lp + jnp.arange(lq)
    kpos = jnp.arange(lp + lq)
    d = qpos[:, None] - kpos[None, :]
    ok = (d >= 0) & (d < WINDOW)
    return ok[None] & ((kpos >= lp)[None, None, :] | prev_ok[:, None, None])


def band_attention(qb, kb, vb, key_ok, sinks):
    B, nb, lq = qb.shape[:3]
    q = qb.reshape(B, nb, lq, SWA_KV_HEADS, SWA_REP, SWA_HEAD_DIM)
    s = jnp.einsum('bnqgrd,bnkgd->bngrqk', q, kb) * (SWA_HEAD_DIM ** -0.5)
    s = jnp.where(key_ok[None, :, None, None], s.astype(F32), -jnp.inf)
    sink = jnp.broadcast_to(sinks.astype(F32).reshape(1, 1, SWA_KV_HEADS, SWA_REP, 1, 1), s.shape[:-1] + (1,))
    p = jax.nn.softmax(jnp.concatenate([s, sink], axis=-1), axis=-1)[..., :-1]
    o = jnp.einsum('bngrqk,bnkgd->bnqgrd', p, vb)
    return o.reshape(B, nb, lq, SWA_HEADS * SWA_HEAD_DIM)


def swa_prompt(q, k, v, sinks):
    B, T = q.shape[:2]
    nb = T // WINDOW
    qb = q.reshape(B, nb, WINDOW, SWA_HEADS, SWA_HEAD_DIM)
    kb = k.reshape(B, nb, WINDOW, SWA_KV_HEADS, SWA_HEAD_DIM)
    vb = v.reshape(B, nb, WINDOW, SWA_KV_HEADS, SWA_HEAD_DIM)
    k_band = jnp.concatenate([jnp.concatenate([jnp.zeros_like(kb[:, :1]), kb[:, :-1]], axis=1), kb], axis=2)
    v_band = jnp.concatenate([jnp.concatenate([jnp.zeros_like(vb[:, :1]), vb[:, :-1]], axis=1), vb], axis=2)
    key_ok = band_mask(WINDOW, WINDOW, jnp.arange(nb) > 0)
    return band_attention(qb, k_band, v_band, key_ok, sinks).reshape(B, T, SWA_HEADS * SWA_HEAD_DIM)


def swa_sample(q, k, v, k_buf, v_buf, sinks):
    B, T = q.shape[:2]
    lp = k_buf.shape[1]
    k_band = jnp.concatenate([k_buf, k], axis=1)[:, None]
    v_band = jnp.concatenate([v_buf, v], axis=1)[:, None]
    key_ok = band_mask(lp, T, jnp.ones((1,), dtype=bool))
    return band_attention(q[:, None], k_band, v_band, key_ok, sinks).reshape(B, T, SWA_HEADS * SWA_HEAD_DIM)


def even_mixer(h, lb, w_in, w_out, lam_re, lam_im, log_dt, b_re, b_im, c_re, c_im, d_skip, glu_w, glu_b,
               hg_norm_g, s5_re0, s5_im0, hg_s0):
    z = jnp.einsum('btd,de->bte', h, w_in).astype(F32)
    u, hq, hf, hi, hgate = split_cols(z, EVEN_SIZES)
    y_a, s5_re, s5_im = s5_mixer(u, lam_re, lam_im, log_dt, b_re, b_im, c_re, c_im, d_skip, glu_w, glu_b,
                                 s5_re0, s5_im0)
    y_b, hg_s = hgrn2_mixer(hq, hf, hi, hgate, lb, hg_norm_g, hg_s0)
    out = jnp.concatenate([y_a, y_b], axis=-1) @ w_out.astype(F32)
    return out, s5_re, s5_im, hg_s


def odd_mixer(h, w_in, w_out, sinks, w_gate2, b_gate, gla_norm_g, k_buf, v_buf, gla_s0):
    B, T, _ = h.shape
    z = jnp.einsum('btd,de->bte', h, w_in).astype(F32)
    qa, ka, va, qd, kd, vd, glr, og = split_cols(z, ODD_SIZES)
    qa = qa.reshape(B, T, SWA_HEADS, SWA_HEAD_DIM)
    ka = ka.reshape(B, T, SWA_KV_HEADS, SWA_HEAD_DIM)
    va = va.reshape(B, T, SWA_KV_HEADS, SWA_HEAD_DIM)
    if k_buf is None:
        attn = swa_prompt(qa, ka, va, sinks)
        keep = min(WINDOW, T)
        new_k, new_v = ka[:, -keep:], va[:, -keep:]
    else:
        kb, vb = k_buf.astype(F32), v_buf.astype(F32)
        attn = swa_sample(qa, ka, va, kb, vb, sinks)
        keep = kb.shape[1]
        new_k = jnp.concatenate([kb, ka], axis=1)[:, -keep:]
        new_v = jnp.concatenate([vb, va], axis=1)[:, -keep:]
    log_f = jax.nn.log_sigmoid(glr @ w_gate2.astype(F32) + b_gate.astype(F32)) / GLA_TAU

    def heads(a, d):
        return a.reshape(B, T, GLA_HEADS, d)

    o, s_last = chunked_gla(heads(qd, GLA_DK) * (GLA_DK ** -0.5), heads(kd, GLA_DK), heads(vd, GLA_DV),
                            heads(log_f, GLA_DK), gla_s0)
    o = rmsnorm(o, gla_norm_g.reshape(GLA_HEADS, GLA_DV)) * jax.nn.silu(heads(og, GLA_DV))
    out = jnp.concatenate([attn, o.reshape(B, T, GLA_VAL_WIDTH)], axis=-1) @ w_out.astype(F32)
    return out, new_k, new_v, s_last


def peer_ffn(h, wq, sub_keys, u_tab, v_tab):
    B, T, D = h.shape
    n = B * T
    pad = (-n) % PEER_BLOCK
    tok = jnp.pad(h.reshape(n, D), ((0, pad), (0, 0)))
    blocks = tok.reshape(-1, PEER_BLOCK, D)
    n_cand = PEER_TOPK * PEER_TOPK

    def one_block(xb):
        q = (xb @ wq).reshape(PEER_BLOCK, PEER_HEADS, 2, PEER_QHALF)
        s = jnp.einsum('thcd,ckd->thck', q, sub_keys).astype(F32)
        sv, si = lax.top_k(s, PEER_TOPK)
        cand = sv[:, :, 0, :, None] + sv[:, :, 1, None, :]
        cidx = si[:, :, 0, :, None] * N_KEYS + si[:, :, 1, None, :]
        top_s, top_pos = lax.top_k(cand.reshape(PEER_BLOCK, PEER_HEADS, n_cand), PEER_TOPK)
        experts = jnp.take_along_axis(cidx.reshape(PEER_BLOCK, PEER_HEADS, n_cand), top_pos, axis=-1)
        gate = jax.nn.softmax(top_s, axis=-1)
        act = jax.nn.gelu(jnp.einsum('td,thkd->thk', xb, u_tab[experts]))
        return jnp.einsum('thk,thkd->td', gate * act, v_tab[experts])

    out = lax.map(one_block, blocks)
    return out.reshape(-1, D)[:n].reshape(B, T, D)


def trunk(x, c, s5_re0, s5_im0, hg0, k_buf0, v_buf0, gla0, w):
    c_act = jax.nn.silu(c.astype(F32))
    lb_all = jnp.cumsum(jax.nn.softmax(w['hgrn_lb_logits'].astype(F32), axis=0), axis=0)
    n_re, n_im, n_hg, n_k, n_v, n_gla = [], [], [], [], [], []
    for l in range(DEPTH):
        mod = c_act @ w['ada_w'][l].astype(F32) + w['ada_b'][l].astype(F32)
        sh1, sc1, g1, sh2, sc2, g2 = jnp.split(mod[:, None, :], 6, axis=-1)
        h = rmsnorm(x, w['norm_mix_g'][l]) * (1.0 + sc1) + sh1
        if l % 2 == 0:
            e = l // 2
            mix, r, i, hs = even_mixer(h, lb_all[l], w['w_in_even'][e], w['w_out_even'][e],
                                       w['s5_lambda_re'][e], w['s5_lambda_im'][e], w['s5_log_dt'][e],
                                       w['s5_b_re'][e], w['s5_b_im'][e], w['s5_c_re'][e], w['s5_c_im'][e],
                                       w['s5_d'][e], w['s5_glu_w'][e], w['s5_glu_b'][e], w['hgrn_norm_g'][e],
                                       s5_re0[e], s5_im0[e], hg0[e])
            n_re.append(r)
            n_im.append(i)
            n_hg.append(hs)
        else:
            o = l // 2
            kb = None if k_buf0 is None else k_buf0[o]
            vb = None if v_buf0 is None else v_buf0[o]
            mix, nk, nv, gs = odd_mixer(h, w['w_in_odd'][o], w['w_out_odd'][o], w['swa_sinks'][o],
                                        w['gla_w_gate2'][o], w['gla_b_gate'][o], w['gla_norm_g'][o],
                                        kb, vb, gla0[o])
            n_k.append(nk)
            n_v.append(nv)
            n_gla.append(gs)
        x = x + g1 * mix
        h = rmsnorm(x, w['norm_ffn_g'][l]) * (1.0 + sc2) + sh2
        x = x + g2 * peer_ffn(h, w['peer_wq'][l], w['peer_keys'][l], w['peer_u'][l], w['peer_v'][l])
    y = rmsnorm(x, w['final_norm_g'])
    return (y, jnp.stack(n_re), jnp.stack(n_im), jnp.stack(n_hg), jnp.stack(n_k), jnp.stack(n_v), jnp.stack(n_gla))


def setup_inputs(seed: int = 0) -> dict:
    key = jax.random.key(seed)
    ks = iter(jax.random.split(key, 64))

    def nrm(shape, scale=1.0):
        return scale * jax.random.normal(next(ks), shape, F32)

    inp = {}
    inp['x_prompt'] = nrm((BATCH, SEQ, D_MODEL))
    inp['x_sample'] = nrm((DEC_BATCH, DEC_SEQ, D_MODEL))
    inp['c_prompt'] = nrm((BATCH, D_MODEL))
    inp['c_sample'] = nrm((DEC_BATCH, D_MODEL))
    inp['state_s5_re'] = nrm((N_EVEN, DEC_BATCH, S5_GROUPS, S5_STATE), 0.5)
    inp['state_s5_im'] = nrm((N_EVEN, DEC_BATCH, S5_GROUPS, S5_STATE), 0.5)
    inp['state_hgrn'] = nrm((N_EVEN, DEC_BATCH, HG_HEADS, HG_HEAD_DIM, HG_HEAD_DIM), 0.3)
    inp['cache_swa_k'] = nrm((N_ODD, DEC_BATCH, W_BUF, SWA_KV_HEADS, SWA_HEAD_DIM))
    inp['cache_swa_v'] = nrm((N_ODD, DEC_BATCH, W_BUF, SWA_KV_HEADS, SWA_HEAD_DIM))
    inp['state_gla'] = nrm((N_ODD, DEC_BATCH, GLA_HEADS, GLA_DK, GLA_DV), 0.3)
    inp['ada_w'] = nrm((DEPTH, D_MODEL, 6 * D_MODEL), 0.5 * D_MODEL ** -0.5)
    inp['ada_b'] = nrm((DEPTH, 6 * D_MODEL), 0.01)
    inp['norm_mix_g'] = 1.0 + nrm((DEPTH, D_MODEL), 0.05)
    inp['norm_ffn_g'] = 1.0 + nrm((DEPTH, D_MODEL), 0.05)
    inp['final_norm_g'] = 1.0 + nrm((D_MODEL,), 0.05)
    inp['w_in_even'] = nrm((N_EVEN, D_MODEL, EVEN_IN), D_MODEL ** -0.5)
    inp['w_out_even'] = nrm((N_EVEN, EVEN_OUT, D_MODEL), EVEN_OUT ** -0.5)
    inp['s5_lambda_re'] = -0.5 + nrm((N_EVEN, S5_GROUPS, S5_STATE), 0.01)
    inp['s5_lambda_im'] = math.pi * jnp.arange(S5_STATE, dtype=F32) + nrm((N_EVEN, S5_GROUPS, S5_STATE), 0.01)
    inp['s5_log_dt'] = jax.random.uniform(next(ks), (N_EVEN, S5_GROUPS), F32, math.log(1e-3), math.log(1e-1))
    inp['s5_b_re'] = nrm((N_EVEN, S5_GROUPS, S5_STATE, S5_GROUP_CH), 0.7 * S5_GROUP_CH ** -0.5)
    inp['s5_b_im'] = nrm((N_EVEN, S5_GROUPS, S5_STATE, S5_GROUP_CH), 0.7 * S5_GROUP_CH ** -0.5)
    inp['s5_c_re'] = nrm((N_EVEN, S5_GROUPS, S5_GROUP_CH, S5_STATE), 0.7 * S5_STATE ** -0.5)
    inp['s5_c_im'] = nrm((N_EVEN, S5_GROUPS, S5_GROUP_CH, S5_STATE), 0.7 * S5_STATE ** -0.5)
    inp['s5_d'] = nrm((N_EVEN, S5_WIDTH))
    inp['s5_glu_w'] = nrm((N_EVEN, S5_WIDTH, S5_WIDTH), S5_WIDTH ** -0.5)
    inp['s5_glu_b'] = nrm((N_EVEN, S5_WIDTH), 0.01)
    inp['hgrn_lb_logits'] = nrm((DEPTH + 1, HG_WIDTH), 0.5)
    inp['hgrn_norm_g'] = 1.0 + nrm((N_EVEN, HG_WIDTH), 0.05)
    inp['w_in_odd'] = nrm((N_ODD, D_MODEL, ODD_IN), D_MODEL ** -0.5)
    inp['w_out_odd'] = nrm((N_ODD, ODD_OUT, D_MODEL), ODD_OUT ** -0.5)
    inp['swa_sinks'] = nrm((N_ODD, SWA_HEADS))
    inp['gla_w_gate2'] = nrm((N_ODD, GLA_RANK, GLA_KEY_WIDTH), GLA_RANK ** -0.5)
    inp['gla_b_gate'] = nrm((N_ODD, GLA_KEY_WIDTH), 0.5)
    inp['gla_norm_g'] = 1.0 + nrm((N_ODD, GLA_VAL_WIDTH), 0.05)
    inp['peer_wq'] = nrm((DEPTH, D_MODEL, PEER_HEADS * PEER_QDIM), D_MODEL ** -0.5)
    inp['peer_keys'] = nrm((DEPTH, 2, N_KEYS, PEER_QHALF), PEER_QHALF ** -0.5)
    inp['peer_u'] = nrm((DEPTH, N_EXPERTS, D_MODEL), D_MODEL ** -0.5)
    inp['peer_v'] = nrm((DEPTH, N_EXPERTS, D_MODEL))
    return inp


def reference(x_prompt, x_sample, c_prompt, c_sample, state_s5_re, state_s5_im, state_hgrn, cache_swa_k,
              cache_swa_v, state_gla, ada_w, ada_b, norm_mix_g, norm_ffn_g, final_norm_g, w_in_even, w_out_even,
              s5_lambda_re, s5_lambda_im, s5_log_dt, s5_b_re, s5_b_im, s5_c_re, s5_c_im, s5_d, s5_glu_w, s5_glu_b,
              hgrn_lb_logits, hgrn_norm_g, w_in_odd, w_out_odd, swa_sinks, gla_w_gate2, gla_b_gate, gla_norm_g,
              peer_wq, peer_keys, peer_u, peer_v):
    w = dict(ada_w=ada_w, ada_b=ada_b, norm_mix_g=norm_mix_g, norm_ffn_g=norm_ffn_g, final_norm_g=final_norm_g,
             w_in_even=w_in_even, w_out_even=w_out_even, s5_lambda_re=s5_lambda_re, s5_lambda_im=s5_lambda_im,
             s5_log_dt=s5_log_dt, s5_b_re=s5_b_re, s5_b_im=s5_b_im, s5_c_re=s5_c_re, s5_c_im=s5_c_im, s5_d=s5_d,
             s5_glu_w=s5_glu_w, s5_glu_b=s5_glu_b, hgrn_lb_logits=hgrn_lb_logits, hgrn_norm_g=hgrn_norm_g,
             w_in_odd=w_in_odd, w_out_odd=w_out_odd, swa_sinks=swa_sinks, gla_w_gate2=gla_w_gate2,
             gla_b_gate=gla_b_gate, gla_norm_g=gla_norm_g, peer_wq=peer_wq, peer_keys=peer_keys,
             peer_u=peer_u, peer_v=peer_v)
    bp = x_prompt.shape[0]
    z_s5 = jnp.zeros((N_EVEN, bp, S5_GROUPS, S5_STATE), F32)
    z_hg = jnp.zeros((N_EVEN, bp, HG_HEADS, HG_HEAD_DIM, HG_HEAD_DIM), F32)
    z_gla = jnp.zeros((N_ODD, bp, GLA_HEADS, GLA_DK, GLA_DV), F32)
    y_p, p_s5_re, p_s5_im, p_hgrn, p_swa_k, p_swa_v, p_gla = trunk(
        x_prompt, c_prompt, z_s5, z_s5, z_hg, None, None, z_gla, w)
    y_s, s_s5_re, s_s5_im, s_hgrn, s_swa_k, s_swa_v, s_gla = trunk(
        x_sample, c_sample, state_s5_re, state_s5_im, state_hgrn, cache_swa_k, cache_swa_v, state_gla, w)
    y_prompt = y_p.astype(x_prompt.dtype)
    y_sample = y_s.astype(x_sample.dtype)
    return (y_prompt, y_sample, p_s5_re, p_s5_im, p_hgrn, p_swa_k, p_swa_v, p_gla,
            s_s5_re, s_s5_im, s_hgrn, s_swa_k, s_swa_v, s_gla)
```

```python
import functools
import math

import jax
import jax.numpy as jnp
import numpy as np
from jax import lax
from jax.experimental import pallas as pl
from jax.experimental.pallas import tpu as pltpu

F32 = jnp.float32
BF16 = jnp.bfloat16

D_MODEL = 1024
DEPTH = 2
HALF = D_MODEL // 2
EPS = 1e-6
S5_GROUPS = 32
S5_GROUP_CH = 16
S5_STATE = 64
S5_FLAT = S5_GROUPS * S5_STATE
HG_HEADS = 4
HG_HEAD_DIM = 128
GLA_HEADS = 4
GLA_DK = 64
GLA_DV = 128
GLA_RANK = 16
GLA_TAU = 16.0
SWA_HEADS = 8
SWA_KV_HEADS = 2
SWA_HEAD_DIM = 64
SWA_REP = SWA_HEADS // SWA_KV_HEADS
WINDOW = 128
PEER_HEADS = 8
PEER_TOPK = 16
N_KEYS = 128
N_EXPERTS = N_KEYS * N_KEYS
PEER_QHALF = 128

LANES = 128
SAMPLE_PAD = 16
VMEM_LIMIT = 56 * 1024 * 1024

NT_DIMS = (((1,), (1,)), ((), ()))
TN_DIMS = (((0,), (0,)), ((), ()))


def _params(*sem):
    return pltpu.CompilerParams(dimension_semantics=sem, vmem_limit_bytes=VMEM_LIMIT)


def _dot(a, b):
    return jnp.dot(a, b, preferred_element_type=F32)


def _gelu(x):
    return 0.5 * x * (1.0 + jnp.tanh(math.sqrt(2.0 / math.pi) * (x + 0.044715 * (x * x * x))))


def _sigmoid(x):
    return 1.0 / (1.0 + jnp.exp(-x))


def _log_sigmoid(x):
    return jnp.minimum(x, 0.0) - jnp.log(1.0 + jnp.exp(-jnp.abs(x)))


def _norm_mod(x, g, sc, sh):
    r = lax.rsqrt(jnp.mean(x * x, axis=-1, keepdims=True) + EPS)
    return x * r * g * (1.0 + sc) + sh


def _split3(x):
    hi = x.astype(BF16)
    r1 = x - hi.astype(F32)
    mid = r1.astype(BF16)
    lo = (r1 - mid.astype(F32)).astype(BF16)
    return hi, mid, lo


def _ada_kernel(c_ref, w_ref, b_ref, o_ref):
    c = c_ref[...]
    a = c * _sigmoid(c)
    o_ref[...] = _dot(a.astype(BF16), w_ref[...].astype(BF16)) + b_ref[...]


def ada_mod(c, ada_w, ada_b):
    rows = c.shape[0]
    tn = 1536
    n = ada_w.shape[-1]
    return pl.pallas_call(
        _ada_kernel,
        grid=(DEPTH, n // tn),
        in_specs=[pl.BlockSpec((rows, D_MODEL), lambda l, j: (0, 0)),
                  pl.BlockSpec((None, D_MODEL, tn), lambda l, j: (l, 0, j)),
                  pl.BlockSpec((None, 1, tn), lambda l, j: (l, 0, j))],
        out_specs=pl.BlockSpec((None, rows, tn), lambda l, j: (l, 0, j)),
        out_shape=jax.ShapeDtypeStruct((DEPTH, rows, n), F32),
        compiler_params=_params("parallel", "parallel"),
        name="ada_mod",
    )(c, ada_w, ada_b.reshape(DEPTH, 1, n))


def _norm_matmul_kernel(x_ref, g_ref, sc_ref, sh_ref, w_ref, o_ref):
    h = _norm_mod(x_ref[...], g_ref[...], sc_ref[...], sh_ref[...])
    o_ref[...] = _dot(h.astype(BF16), w_ref[...])


def _mod_spec(mod, tm):
    if mod.shape[1] == 1:
        return pl.BlockSpec((None, 1, D_MODEL), lambda b, i: (b, 0, 0))
    return pl.BlockSpec((None, tm, D_MODEL), lambda b, i: (b, i, 0))


def norm_matmul(x, g, sc, sh, w, tm):
    bsz, t, _ = x.shape
    n = w.shape[1]
    return pl.pallas_call(
        _norm_matmul_kernel,
        grid=(bsz, t // tm),
        in_specs=[pl.BlockSpec((None, tm, D_MODEL), lambda b, i: (b, i, 0)),
                  pl.BlockSpec((1, D_MODEL), lambda b, i: (0, 0)),
                  _mod_spec(sc, tm), _mod_spec(sh, tm),
                  pl.BlockSpec((D_MODEL, n), lambda b, i: (0, 0))],
        out_specs=pl.BlockSpec((None, tm, n), lambda b, i: (b, i, 0)),
        out_shape=jax.ShapeDtypeStruct((bsz, t, n), F32),
        compiler_params=_params("parallel", "parallel"),
        name="norm_matmul",
    )(x, g.reshape(1, D_MODEL), sc, sh, w)


def _out_proj_kernel(a1_ref, a2_ref, w_ref, x_ref, gate_ref, o_ref):
    k1 = a1_ref.shape[-1]
    mix = _dot(a1_ref[...].astype(BF16), w_ref[:k1, :]) + _dot(a2_ref[...].astype(BF16), w_ref[k1:, :])
    o_ref[...] = x_ref[...] + gate_ref[...] * mix


def out_proj(a1, a2, w, x, gate, tm):
    bsz, t, _ = x.shape
    k1, k2 = a1.shape[-1], a2.shape[-1]
    return pl.pallas_call(
        _out_proj_kernel,
        grid=(bsz, t // tm),
        in_specs=[pl.BlockSpec((None, tm, k1), lambda b, i: (b, i, 0)),
                  pl.BlockSpec((None, tm, k2), lambda b, i: (b, i, 0)),
                  pl.BlockSpec((k1 + k2, D_MODEL), lambda b, i: (0, 0)),
                  pl.BlockSpec((None, tm, D_MODEL), lambda b, i: (b, i, 0)),
                  _mod_spec(gate, tm)],
        out_specs=pl.BlockSpec((None, tm, D_MODEL), lambda b, i: (b, i, 0)),
        out_shape=jax.ShapeDtypeStruct((bsz, t, D_MODEL), F32),
        compiler_params=_params("parallel", "parallel"),
        name="out_proj",
    )(a1, a2, w, x, gate)


def _final_norm_kernel(x_ref, g_ref, o_ref):
    x = x_ref[...]
    o_ref[...] = x * lax.rsqrt(jnp.mean(x * x, axis=-1, keepdims=True) + EPS) * g_ref[...]


def final_norm(x, g, tm):
    bsz, t, _ = x.shape
    return pl.pallas_call(
        _final_norm_kernel,
        grid=(bsz, t // tm),
        in_specs=[pl.BlockSpec((None, tm, D_MODEL), lambda b, i: (b, i, 0)),
                  pl.BlockSpec((1, D_MODEL), lambda b, i: (0, 0))],
        out_specs=pl.BlockSpec((None, tm, D_MODEL), lambda b, i: (b, i, 0)),
        out_shape=jax.ShapeDtypeStruct((bsz, t, D_MODEL), F32),
        compiler_params=_params("parallel", "parallel"),
        name="final_norm",
    )(x, g.reshape(1, D_MODEL))


def _cmul_add(ar, ai, hr, hi, br, bi):
    return ar * hr - ai * hi + br, ar * hi + ai * hr + bi


def _s5_kernel(u_ref, bblk_ref, cblk_ref, lam_ref, d_ref, gw_ref, gb_ref, h0_ref,
               y_ref, hlast_ref, h_scr, hp_scr, carry_scr, *, stride, chained):
    nc = u_ref.shape[1]
    p2 = S5_FLAT
    for tau in range(stride):
        h_scr[tau] = _dot(u_ref[tau].astype(BF16), bblk_ref[...])
    l1r, l1i = lam_ref[0:1, :p2], lam_ref[0:1, p2:]

    if chained:
        i = pl.program_id(1)

        @pl.when(i == 0)
        def _():
            carry_scr[...] = h0_ref[...]

        hr, hi = h_scr[0, :, :p2], h_scr[0, :, p2:]
        for tau in range(1, stride):
            hr, hi = _cmul_add(l1r, l1i, hr, hi, h_scr[tau, :, :p2], h_scr[tau, :, p2:])
            h_scr[tau, :, :p2] = hr
            h_scr[tau, :, p2:] = hi
        llr, lli = lam_ref[stride - 1:stride, :p2], lam_ref[stride - 1:stride, p2:]

        cr, ci = carry_scr[:, :p2], carry_scr[:, p2:]
        for c in range(nc):
            hp_scr[c:c + 1, :p2] = cr
            hp_scr[c:c + 1, p2:] = ci
            cr, ci = _cmul_add(llr, lli, cr, ci, h_scr[stride - 1, c:c + 1, :p2], h_scr[stride - 1, c:c + 1, p2:])
        carry_scr[:, :p2] = cr
        carry_scr[:, p2:] = ci
        hpr, hpi = hp_scr[:, :p2], hp_scr[:, p2:]
        for tau in range(stride):
            pr, pi = lam_ref[tau:tau + 1, :p2], lam_ref[tau:tau + 1, p2:]
            hr, hi = _cmul_add(pr, pi, hpr, hpi, h_scr[tau, :, :p2], h_scr[tau, :, p2:])
            h_scr[tau, :, :p2] = hr
            h_scr[tau, :, p2:] = hi
        hlast_ref[:, :p2] = cr
        hlast_ref[:, p2:] = ci
    else:
        hr, hi = h0_ref[:, :p2], h0_ref[:, p2:]
        for tau in range(stride):
            hr, hi = _cmul_add(l1r, l1i, hr, hi, h_scr[tau, :, :p2], h_scr[tau, :, p2:])
            h_scr[tau, :, :p2] = hr
            h_scr[tau, :, p2:] = hi
        hlast_ref[:, :p2] = hr
        hlast_ref[:, p2:] = hi

    for tau in range(stride):
        y = _dot(h_scr[tau].astype(BF16), cblk_ref[...]) + d_ref[...] * u_ref[tau]
        g = _gelu(y)
        y_ref[tau] = g * _sigmoid(_dot(g.astype(BF16), gw_ref[...]) + gb_ref[...])


def s5_layer(u, prm, h0, *, stride, chained, nc):
    bsz, t, _ = u.shape
    chunks = t // stride
    u_de = u.reshape(bsz, chunks, stride, HALF).transpose(0, 2, 1, 3)
    hrows = 1 if chained else nc
    const = lambda b, i: (0, 0)
    kern = functools.partial(_s5_kernel, stride=stride, chained=chained)
    y_de, hlast = pl.pallas_call(
        kern,
        grid=(bsz, chunks // nc),
        in_specs=[pl.BlockSpec((None, stride, nc, HALF), lambda b, i: (b, 0, i, 0)),
                  pl.BlockSpec((HALF, 2 * S5_FLAT), const),
                  pl.BlockSpec((2 * S5_FLAT, HALF), const),
                  pl.BlockSpec((stride, 2 * S5_FLAT), const),
                  pl.BlockSpec((1, HALF), const),
                  pl.BlockSpec((HALF, HALF), const),
                  pl.BlockSpec((1, HALF), const),
                  pl.BlockSpec((None, hrows, 2 * S5_FLAT), lambda b, i: (b, 0, 0))],
        out_specs=[pl.BlockSpec((None, stride, nc, HALF), lambda b, i: (b, 0, i, 0)),
                   pl.BlockSpec((None, hrows, 2 * S5_FLAT), lambda b, i: (b, 0, 0))],
        out_shape=[jax.ShapeDtypeStruct((bsz, stride, chunks, HALF), F32),
                   jax.ShapeDtypeStruct((bsz, hrows, 2 * S5_FLAT), F32)],
        scratch_shapes=[pltpu.VMEM((stride, nc, 2 * S5_FLAT), F32),
                        pltpu.VMEM((nc, 2 * S5_FLAT), F32),
                        pltpu.VMEM((1, 2 * S5_FLAT), F32)],
        compiler_params=_params("parallel", "arbitrary"),
        name="s5_chained" if chained else "s5_batched",
    )(u_de, prm["bblk"], prm["cblk"], prm["lam_pow"][:stride], prm["d"], prm["glu_w"], prm["glu_b"], h0)
    return y_de.transpose(0, 2, 1, 3).reshape(bsz, t, HALF), hlast


def s5_params(lam_re, lam_im, log_dt, b_re, b_im, c_re, c_im, d_skip, glu_w, glu_b, max_stride):
    dt = jnp.exp(log_dt)[:, None]
    n = jnp.arange(1, max_stride + 1, dtype=F32)[:, None, None]
    mag = jnp.exp(n * (lam_re * dt)[None])
    ang = n * (lam_im * dt)[None]
    pow_re, pow_im = mag * jnp.cos(ang), mag * jnp.sin(ang)
    lam_pow = jnp.concatenate([pow_re.reshape(max_stride, S5_FLAT), pow_im.reshape(max_stride, S5_FLAT)], axis=1)
    den = lam_re * lam_re + lam_im * lam_im
    nr, ni = pow_re[0] - 1.0, pow_im[0]
    cr = (nr * lam_re + ni * lam_im) / den
    ci = (ni * lam_re - nr * lam_im) / den
    bb_re = cr[..., None] * b_re - ci[..., None] * b_im
    bb_im = cr[..., None] * b_im + ci[..., None] * b_re
    eye = jnp.eye(S5_GROUPS, dtype=F32)
    pack_b = lambda m: jnp.einsum("gpc,gh->gchp", m, eye).reshape(HALF, S5_FLAT)
    pack_c = lambda m: jnp.einsum("gcp,gh->gphc", m, eye).reshape(S5_FLAT, HALF)
    return dict(
        bblk=jnp.concatenate([pack_b(bb_re), pack_b(bb_im)], axis=1).astype(BF16),
        cblk=jnp.concatenate([pack_c(c_re), -pack_c(c_im)], axis=0).astype(BF16),
        lam_pow=lam_pow, d=d_skip.reshape(1, HALF), glu_w=glu_w.astype(BF16), glu_b=glu_b.reshape(1, HALF))


def _seg_matrices(length):
    t = np.arange(length)[:, None]
    r = np.arange(length)[None, :]
    pre, suf = [], []
    m = 2
    while m <= length:
        same = (t // m) == (r // m)
        pre.append(same & (r <= t))
        suf.append(same & (r > t))
        m *= 2
    return np.concatenate(pre + suf, axis=0).astype(np.float32)


def _pair_level(length):
    t = np.arange(length)[:, None]
    s = np.arange(length)[None, :]
    x = np.where(t > s, t ^ s, 0)
    lvl = np.where(x > 0, np.floor(np.log2(np.maximum(x, 1))), -1)
    return lvl.astype(np.int32)


def _glr_kernel(*refs, mode, heads, dk, dv, length, t_valid):
    if mode == "hgrn":
        (q_ref, f_ref, v_ref, gate_ref, lb_ref, ng_ref, seg_ref, lvl_ref, s0_ref,
         o_ref, slast_ref, s_scr) = refs
    else:
        (q_ref, k_ref, v_ref, gate_ref, glr_ref, w2_ref, b2_ref, ng_ref, seg_ref, lvl_ref, s0_ref,
         o_ref, slast_ref, s_scr) = refs
    i = pl.program_id(1)

    @pl.when(i == 0)
    def _():
        s_scr[...] = s0_ref[...]

    if mode == "hgrn":
        lb = lb_ref[...]
        hf = f_ref[...]
        q = q_ref[...]
        k = (1.0 - lb) * _sigmoid(-hf)
        g = jnp.log(lb + (1.0 - lb) * _sigmoid(hf))
        gate = _sigmoid(gate_ref[...])
    else:
        q = q_ref[...] * (dk ** -0.5)
        k = k_ref[...]
        g = _log_sigmoid(_dot(glr_ref[...].astype(BF16), w2_ref[...]) + b2_ref[...]) * (1.0 / GLA_TAU)
        og = gate_ref[...]
        gate = og * _sigmoid(og)
    if t_valid < length:
        live = lax.broadcasted_iota(jnp.int32, (length, 1), 0) < t_valid
        g = jnp.where(live, g, 0.0)
        k = jnp.where(live, k, 0.0)
    v = v_ref[...]

    n_lv = int(math.log2(length))
    seg = seg_ref[...]
    pieces = _split3(g)
    ps = _dot(seg, pieces[0]) + _dot(seg, pieces[1]) + _dot(seg, pieces[2])
    lvl = lvl_ref[...]
    ones = jnp.ones((length, dv), BF16)

    def prefix(j):
        return ps[j * length:(j + 1) * length]

    def suffix(j):
        return ps[(n_lv + j) * length:(n_lv + j + 1) * length]

    for h in range(heads):
        cs = slice(h * dk, (h + 1) * dk)
        vs = slice(h * dv, (h + 1) * dv)
        qh, kh, gh, vh = q[:, cs], k[:, cs], g[:, cs], v[:, vs]
        vb = vh.astype(BF16)
        scores = jnp.zeros((length, length), F32)
        for j in range(n_lv):
            if j == 0:
                qe, ke = qh * jnp.exp(gh), kh
            else:
                qe, ke = qh * jnp.exp(prefix(j - 1)[:, cs]), kh * jnp.exp(suffix(j - 1)[:, cs])
            sc = lax.dot_general(qe.astype(BF16), ke.astype(BF16), NT_DIMS, preferred_element_type=F32)
            scores = scores + jnp.where(lvl == j, sc, 0.0)
        diag = jnp.sum(qh * kh, axis=-1, keepdims=True)
        s_old = s_scr[h]
        q_in = qh * jnp.exp(prefix(n_lv - 1)[:, cs])
        o = _dot(scores.astype(BF16), vb) + diag * vh + _dot(q_in.astype(BF16), s_old.astype(BF16))
        k_out = (kh * jnp.exp(suffix(n_lv - 1)[:, cs])).astype(BF16)
        total = sum(lax.dot_general(p[:, cs], ones, TN_DIMS, preferred_element_type=F32) for p in pieces)
        s_new = jnp.exp(total) * s_old + lax.dot_general(k_out, vb, TN_DIMS, preferred_element_type=F32)
        s_scr[h] = s_new
        slast_ref[h] = s_new
        o = o * lax.rsqrt(jnp.mean(o * o, axis=-1, keepdims=True) + EPS) * ng_ref[:, vs] * gate[:, vs]
        o_ref[:, vs] = o


def gated_linear(z, mode, prm, s0, *, length, t_valid):
    bsz, t, _ = z.shape
    heads = HG_HEADS
    dk, dv = (HG_HEAD_DIM, HG_HEAD_DIM) if mode == "hgrn" else (GLA_DK, GLA_DV)
    const = lambda b, i: (0, 0)
    col = lambda width, j: pl.BlockSpec((None, length, width), lambda b, i: (b, i, j))
    seg = jnp.asarray(_seg_matrices(length), BF16)
    lvl = jnp.asarray(_pair_level(length))
    if mode == "hgrn":
        ins = [z, z, z, z, prm["lb"], prm["norm_g"]]
        specs = [col(HALF, 1), col(HALF, 2), col(HALF, 3), col(HALF, 4),
                 pl.BlockSpec((1, HALF), const), pl.BlockSpec((1, HALF), const)]
    else:
        ins = [z, z, z, z, z, prm["w2"], prm["b2"], prm["norm_g"]]
        specs = [col(HALF // 2, 6), col(HALF // 2, 7), col(HALF, 1), col(HALF, 2), col(LANES, 18),
                 pl.BlockSpec((LANES, HALF // 2), const), pl.BlockSpec((1, HALF // 2), const),
                 pl.BlockSpec((1, HALF), const)]
    ins += [seg, lvl, s0]
    specs += [pl.BlockSpec(seg.shape, const), pl.BlockSpec(lvl.shape, const),
              pl.BlockSpec((None, heads, dk, dv), lambda b, i: (b, 0, 0, 0))]
    kern = functools.partial(_glr_kernel, mode=mode, heads=heads, dk=dk, dv=dv, length=length, t_valid=t_valid)
    return pl.pallas_call(
        kern,
        grid=(bsz, t // length),
        in_specs=specs,
        out_specs=[pl.BlockSpec((None, length, heads * dv), lambda b, i: (b, i, 0)),
                   pl.BlockSpec((None, heads, dk, dv), lambda b, i: (b, 0, 0, 0))],
        out_shape=[jax.ShapeDtypeStruct((bsz, t, heads * dv), F32),
                   jax.ShapeDtypeStruct((bsz, heads, dk, dv), F32)],
        scratch_shapes=[pltpu.VMEM((heads, dk, dv), F32)],
        compiler_params=_params("parallel", "arbitrary"),
        name="gated_linear_" + mode,
    )(*ins)


def _swa_kernel(q_ref, kc_ref, vc_ref, kp_ref, vp_ref, sink_ref, o_ref, *, lq, first_block_has_no_prev):
    hd = SWA_HEAD_DIM
    rows = SWA_REP * lq
    jp = lax.broadcasted_iota(jnp.int32, (rows, WINDOW), 0) & (lq - 1)
    if first_block_has_no_prev:
        jp = jp + jnp.where(pl.program_id(1) > 0, 0, WINDOW)
    ok_prev = lax.broadcasted_iota(jnp.int32, (rows, WINDOW), 1) > jp
    jc = lax.broadcasted_iota(jnp.int32, (rows, lq), 0) & (lq - 1)
    ok_cur = lax.broadcasted_iota(jnp.int32, (rows, lq), 1) <= jc
    head_of_row = lax.broadcasted_iota(jnp.int32, (rows, 1), 0) >> int(math.log2(lq))
    q = q_ref[...]
    scale = hd ** -0.5
    for g in range(SWA_KV_HEADS):
        ks = slice(g * hd, (g + 1) * hd)
        qs = jnp.concatenate([q[:, (g * SWA_REP + rr) * hd:(g * SWA_REP + rr + 1) * hd]
                              for rr in range(SWA_REP)], axis=0).astype(BF16)
        sp = lax.dot_general(qs, kp_ref[:, ks].astype(BF16), NT_DIMS, preferred_element_type=F32) * scale
        sc = lax.dot_general(qs, kc_ref[:, ks].astype(BF16), NT_DIMS, preferred_element_type=F32) * scale
        sp = jnp.where(ok_prev, sp, -jnp.inf)
        sc = jnp.where(ok_cur, sc, -jnp.inf)
        sink = jnp.zeros((rows, 1), F32)
        for rr in range(SWA_REP):
            hsel = g * SWA_REP + rr
            sink = jnp.where(head_of_row == rr, sink_ref[hsel:hsel + 1, 0:1], sink)
        m = jnp.maximum(jnp.maximum(jnp.max(sp, axis=-1, keepdims=True), jnp.max(sc, axis=-1, keepdims=True)), sink)
        pp = jnp.exp(sp - m)
        pc = jnp.exp(sc - m)
        den = jnp.sum(pp, axis=-1, keepdims=True) + jnp.sum(pc, axis=-1, keepdims=True) + jnp.exp(sink - m)
        o = (_dot(pp.astype(BF16), vp_ref[:, ks].astype(BF16)) + _dot(pc.astype(BF16), vc_ref[:, ks].astype(BF16))) / den
        for rr in range(SWA_REP):
            hcol = (g * SWA_REP + rr) * hd
            o_ref[:, hcol:hcol + hd] = o[rr * lq:(rr + 1) * lq]


def swa_attention(z, k_prev, v_prev, sinks, *, lq, prompt):
    bsz, t, _ = z.shape
    kvw = SWA_KV_HEADS * SWA_HEAD_DIM
    cur = lambda j: pl.BlockSpec((None, lq, kvw), lambda b, i: (b, i, j))
    if prompt:
        prev = lambda j: pl.BlockSpec((None, lq, kvw), lambda b, i: (b, jnp.maximum(i - 1, 0), j))
        kp, vp, prev_specs = z, z, [prev(16), prev(17)]
    else:
        kp, vp = k_prev, v_prev
        prev_specs = [pl.BlockSpec((None, WINDOW, kvw), lambda b, i: (b, 0, 0))] * 2
    sink_arr = jnp.broadcast_to(sinks.reshape(SWA_HEADS, 1), (SWA_HEADS, LANES))
    kern = functools.partial(_swa_kernel, lq=lq, first_block_has_no_prev=prompt)
    return pl.pallas_call(
        kern,
        grid=(bsz, t // lq),
        in_specs=[pl.BlockSpec((None, lq, HALF), lambda b, i: (b, i, 0)), cur(16), cur(17)] + prev_specs
                 + [pl.BlockSpec((SWA_HEADS, LANES), lambda b, i: (0, 0))],
        out_specs=pl.BlockSpec((None, lq, HALF), lambda b, i: (b, i, 0)),
        out_shape=jax.ShapeDtypeStruct((bsz, t, HALF), F32),
        compiler_params=_params("parallel", "parallel"),
        name="swa_prompt" if prompt else "swa_sample",
    )(z, z, z, kp, vp, sink_arr)


def _top16(s):
    key_iota = lax.broadcasted_iota(jnp.int32, s.shape, 0).astype(F32)
    row_iota = lax.broadcasted_iota(jnp.int32, (PEER_TOPK, s.shape[1]), 0)

    def body(a, carry):
        cur, rank, vals = carry
        m = jnp.max(cur, axis=0, keepdims=True)
        idx = jnp.min(jnp.where(cur == m, key_iota, float(N_KEYS)), axis=0, keepdims=True)
        hit = key_iota == idx
        rank = jnp.where(hit, a.astype(F32), rank)
        vals = jnp.where(row_iota == a, m, vals)
        return jnp.where(hit, -jnp.inf, cur), rank, vals

    init = (s, jnp.full(s.shape, float(PEER_TOPK), F32), jnp.zeros((PEER_TOPK, s.shape[1]), F32))
    _, rank, vals = lax.fori_loop(0, PEER_TOPK, body, init)
    return rank, vals


def _merge16(v0, v1):
    lanes = v0.shape[1]
    a_iota = lax.broadcasted_iota(jnp.int32, (PEER_TOPK, lanes), 0).astype(F32)
    top = v0[0:1] + v1[0:1]

    def body(_, carry):
        front, count, z = carry
        m = jnp.max(front, axis=0, keepdims=True)
        a_star = jnp.min(jnp.where(front == m, a_iota, float(PEER_TOPK)), axis=0, keepdims=True)
        hit = a_iota == a_star
        count = jnp.where(hit, count + 1.0, count)
        cn = jnp.sum(jnp.where(hit, count, 0.0), axis=0, keepdims=True)
        nxt = jnp.sum(jnp.where(a_iota == cn, v1, 0.0), axis=0, keepdims=True)
        front = jnp.where(hit, jnp.where(cn < float(PEER_TOPK), v0 + nxt, -jnp.inf), front)
        return front, count, z + jnp.exp(m - top)

    init = (v0 + v1[0:1], jnp.zeros((PEER_TOPK, lanes), F32), jnp.zeros((1, lanes), F32))
    _, count, z = lax.fori_loop(0, PEER_TOPK, body, init)
    return count, z


def _peer_topk_kernel(x_ref, g_ref, sc_ref, sh_ref, wqt_ref, keys_ref,
                      ht_ref, rank1_ref, e1_ref, cnt0_ref, e0_ref, qt_scr, s_scr):
    tt = x_ref.shape[0]
    h = _norm_mod(x_ref[...], g_ref[...], sc_ref[...], sh_ref[...])
    htb = h.T.astype(BF16)
    ht_ref[...] = htb
    qt_scr[...] = _dot(wqt_ref[...], htb)
    for hh in range(PEER_HEADS):
        for c in range(2):
            r0 = (hh * 2 + c) * PEER_QHALF
            s_scr[c] = _dot(keys_ref[c], qt_scr[r0:r0 + PEER_QHALF, :].astype(BF16))
        for lg in range(tt // LANES):
            ls = slice(lg * LANES, (lg + 1) * LANES)
            s0, s1 = s_scr[0, :, ls], s_scr[1, :, ls]
            rank0, v0 = _top16(s0)
            rank1, v1 = _top16(s1)
            count, z = _merge16(v0, v1)
            cnt0 = jnp.zeros_like(s0)
            for a in range(PEER_TOPK):
                cnt0 = cnt0 + jnp.where(rank0 == float(a), count[a:a + 1], 0.0)
            rank1_ref[hh, :, ls] = rank1
            e1_ref[hh, :, ls] = jnp.exp(s1 - v1[0:1])
            cnt0_ref[hh, :, ls] = cnt0
            e0_ref[hh, :, ls] = jnp.exp(s0 - v0[0:1]) / z


def peer_topk(x, g, sc, sh, wqt, keys, tt):
    bsz, t, _ = x.shape
    n = bsz * t
    nq = wqt.shape[0]
    tok = lambda b, i: (0, 0, b * (t // tt) + i)
    stat = pl.BlockSpec((PEER_HEADS, N_KEYS, tt), tok)
    stat_shape = jax.ShapeDtypeStruct((PEER_HEADS, N_KEYS, n), F32)
    return pl.pallas_call(
        _peer_topk_kernel,
        grid=(bsz, t // tt),
        in_specs=[pl.BlockSpec((None, tt, D_MODEL), lambda b, i: (b, i, 0)),
                  pl.BlockSpec((1, D_MODEL), lambda b, i: (0, 0)),
                  _mod_spec(sc, tt), _mod_spec(sh, tt),
                  pl.BlockSpec((nq, D_MODEL), lambda b, i: (0, 0)),
                  pl.BlockSpec((2, N_KEYS, PEER_QHALF), lambda b, i: (0, 0, 0))],
        out_specs=[pl.BlockSpec((D_MODEL, tt), lambda b, i: (0, b * (t // tt) + i)), stat, stat, stat, stat],
        out_shape=[jax.ShapeDtypeStruct((D_MODEL, n), BF16), stat_shape, stat_shape, stat_shape, stat_shape],
        scratch_shapes=[pltpu.VMEM((nq, tt), F32), pltpu.VMEM((2, N_KEYS, tt), F32)],
        compiler_params=_params("parallel", "parallel"),
        name="peer_topk",
    )(x, g.reshape(1, D_MODEL), sc, sh, wqt, keys)


def _peer_dense_kernel(ht_ref, rank1_ref, e1_ref, cnt0_ref, e0_ref, u_ref, vt_ref, x_ref, gate_ref,
                       o_ref, acc_scr, a_scr, h_scr):
    e = pl.program_id(2)
    et, tt = a_scr.shape
    nb = et // N_KEYS

    @pl.when(e == 0)
    def _():
        acc_scr[...] = jnp.zeros_like(acc_scr)

    a_scr[...] = _dot(u_ref[...], ht_ref[...])

    for ib in range(nb):
        rs = slice(ib * N_KEYS, (ib + 1) * N_KEYS)
        for lg in range(tt // LANES):
            ls = slice(lg * LANES, (lg + 1) * LANES)
            w = jnp.zeros((N_KEYS, LANES), F32)
            for hh in range(PEER_HEADS):
                cnt = cnt0_ref[hh, ib:ib + 1, ls]
                e0 = e0_ref[hh, ib:ib + 1, ls]
                w = w + jnp.where(rank1_ref[hh, :, ls] < cnt, e1_ref[hh, :, ls] * e0, 0.0)
            h_scr[rs, ls] = (_gelu(a_scr[rs, ls]) * w).astype(BF16)
    acc_scr[...] += _dot(vt_ref[...], h_scr[...])

    @pl.when(e == pl.num_programs(2) - 1)
    def _():
        o_ref[...] = x_ref[...] + gate_ref[...] * acc_scr[...].T


def peer_dense(ht, stats, u_bf, vt_bf, x, gate, tt, et):
    bsz, t, _ = x.shape
    tok = lambda b, i, e: (0, 0, b * (t // tt) + i)
    stat = pl.BlockSpec((PEER_HEADS, N_KEYS, tt), tok)
    if gate.shape[1] == 1:
        gate_spec = pl.BlockSpec((None, 1, D_MODEL), lambda b, i, e: (b, 0, 0))
    else:
        gate_spec = pl.BlockSpec((None, tt, D_MODEL), lambda b, i, e: (b, i, 0))
    return pl.pallas_call(
        _peer_dense_kernel,
        grid=(bsz, t // tt, N_EXPERTS // et),
        in_specs=[pl.BlockSpec((D_MODEL, tt), lambda b, i, e: (0, b * (t // tt) + i)),
                  stat, stat,
                  pl.BlockSpec((PEER_HEADS, et // N_KEYS, tt), lambda b, i, e: (0, e, b * (t // tt) + i)),
                  pl.BlockSpec((PEER_HEADS, et // N_KEYS, tt), lambda b, i, e: (0, e, b * (t // tt) + i)),
                  pl.BlockSpec((et, D_MODEL), lambda b, i, e: (e, 0)),
                  pl.BlockSpec((D_MODEL, et), lambda b, i, e: (0, e)),
                  pl.BlockSpec((None, tt, D_MODEL), lambda b, i, e: (b, i, 0)),
                  gate_spec],
        out_specs=pl.BlockSpec((None, tt, D_MODEL), lambda b, i, e: (b, i, 0)),
        out_shape=jax.ShapeDtypeStruct((bsz, t, D_MODEL), F32),
        scratch_shapes=[pltpu.VMEM((D_MODEL, tt), F32), pltpu.VMEM((et, tt), F32), pltpu.VMEM((et, tt), BF16)],
        compiler_params=_params("parallel", "parallel", "arbitrary"),
        name="peer_dense",
    )(ht, *stats, u_bf, vt_bf, x, gate)


def peer_layer(x, g, sc, sh, gate, w, tt, et):
    ht, *stats = peer_topk(x, g, sc, sh, w["wqt"], w["keys"], tt)
    return peer_dense(ht, stats, w["u"], w["vt"], x, gate, tt, et)


def _odd_in_weight(w_in_odd):
    qa, ka, va, qd, kd, vd, glr, og = jnp.split(w_in_odd, [512, 640, 768, 1024, 1280, 1792, 1808], axis=1)
    glr = jnp.pad(glr, ((0, 0), (0, LANES - GLA_RANK)))
    return jnp.concatenate([qa, vd, og, qd, kd, ka, va, glr], axis=1).astype(BF16)


def _prepare_weights(w):
    p = dict(w)
    p["lb0"] = jax.nn.softmax(w["hgrn_lb_logits"], axis=0)[0].reshape(1, HALF)
    p["w_in_even_bf"] = w["w_in_even"][0].astype(BF16)
    p["w_out_even_bf"] = w["w_out_even"][0].astype(BF16)
    p["w_in_odd_bf"] = _odd_in_weight(w["w_in_odd"][0])
    p["w_out_odd_bf"] = w["w_out_odd"][0].astype(BF16)
    p["s5"] = s5_params(w["s5_lambda_re"][0], w["s5_lambda_im"][0], w["s5_log_dt"][0], w["s5_b_re"][0],
                        w["s5_b_im"][0], w["s5_c_re"][0], w["s5_c_im"][0], w["s5_d"][0], w["s5_glu_w"][0],
                        w["s5_glu_b"][0], 8)
    p["hgrn"] = dict(lb=p["lb0"], norm_g=w["hgrn_norm_g"][0].reshape(1, HALF))
    p["gla"] = dict(w2=jnp.pad(w["gla_w_gate2"][0], ((0, LANES - GLA_RANK), (0, 0))).astype(BF16),
                    b2=w["gla_b_gate"][0].reshape(1, HALF // 2), norm_g=w["gla_norm_g"][0].reshape(1, HALF))
    p["peer"] = [dict(wqt=w["peer_wq"][l].T.astype(BF16), keys=w["peer_keys"][l].astype(BF16),
                      u=w["peer_u"][l].astype(BF16), vt=w["peer_v"][l].T.astype(BF16)) for l in range(DEPTH)]
    return p


def _trunk(x, mod, p, *, sample, s5_h0, hg_s0, k_buf, v_buf, gla_s0):
    bsz, t, _ = x.shape
    rows = bsz * t
    if sample:
        xr = x.reshape(1, rows, D_MODEL)
        tm = rows
        expand = lambda m: jnp.repeat(m, t, axis=0).reshape(1, rows, D_MODEL)
    else:
        xr = x
        tm = 512
        expand = lambda m: m.reshape(bsz, 1, D_MODEL)
    out = {}
    for l in range(DEPTH):
        sh1, sc1, g1, sh2, sc2, g2 = [expand(m) for m in jnp.split(mod[l], 6, axis=-1)]
        if l % 2 == 0:
            z = norm_matmul(xr, p["norm_mix_g"][l], sc1, sh1, p["w_in_even_bf"], tm)
            if sample:
                ya, hl = s5_layer(z[..., :HALF], p["s5"], s5_h0.reshape(1, bsz, 2 * S5_FLAT), stride=t, chained=False, nc=bsz)
                zp = jnp.pad(z.reshape(bsz, t, -1), ((0, 0), (0, SAMPLE_PAD - t), (0, 0)))
                yb, hs = gated_linear(zp, "hgrn", p["hgrn"], hg_s0, length=SAMPLE_PAD,t_valid=t)
                yb = yb[:, :t].reshape(1, rows, HALF)
            else:
                ya, hl = s5_layer(z[..., :HALF], p["s5"], s5_h0.reshape(bsz, 1, 2 * S5_FLAT), stride=8, chained=True, nc=32)
                yb, hs = gated_linear(z, "hgrn", p["hgrn"], hg_s0, length=128, t_valid=128)
            hl = hl.reshape(bsz, 2 * S5_FLAT)
            out["s5_re"] = hl[:, :S5_FLAT].reshape(1, bsz, S5_GROUPS, S5_STATE)
            out["s5_im"] = hl[:, S5_FLAT:].reshape(1, bsz, S5_GROUPS, S5_STATE)
            out["hgrn"] = hs[None]
            xr = out_proj(ya, yb, p["w_out_even_bf"], xr, g1, tm)
        else:
            z = norm_matmul(xr, p["norm_mix_g"][l], sc1, sh1, p["w_in_odd_bf"], tm)
            kvw = SWA_KV_HEADS * SWA_HEAD_DIM
            if sample:
                z3 = z.reshape(bsz, t, -1)
                zp = jnp.pad(z3, ((0, 0), (0, SAMPLE_PAD - t), (0, 0)))
                kb = k_buf.reshape(bsz, WINDOW, kvw)
                vb = v_buf.reshape(bsz, WINDOW, kvw)
                attn = swa_attention(zp, kb, vb, p["swa_sinks"][0], lq=SAMPLE_PAD, prompt=False)[:, :t]
                attn = attn.reshape(1, rows, HALF)
                new_k = jnp.concatenate([kb, z3[:, :, 2048:2048 + kvw]], axis=1)[:, -WINDOW:]
                new_v = jnp.concatenate([vb, z3[:, :, 2048 + kvw:2048 + 2 * kvw]], axis=1)[:, -WINDOW:]
                yb, gs = gated_linear(zp, "gla", p["gla"], gla_s0, length=SAMPLE_PAD,t_valid=t)
                yb = yb[:, :t].reshape(1, rows, HALF)
            else:
                attn = swa_attention(z, None, None, p["swa_sinks"][0], lq=WINDOW, prompt=True)
                new_k = z[:, -WINDOW:, 2048:2048 + kvw]
                new_v = z[:, -WINDOW:, 2048 + kvw:2048 + 2 * kvw]
                yb, gs = gated_linear(z, "gla", p["gla"], gla_s0, length=128, t_valid=128)
            out["swa_k"] = new_k.reshape(1, bsz, WINDOW, SWA_KV_HEADS, SWA_HEAD_DIM)
            out["swa_v"] = new_v.reshape(1, bsz, WINDOW, SWA_KV_HEADS, SWA_HEAD_DIM)
            out["gla"] = gs[None]
            xr = out_proj(attn, yb, p["w_out_odd_bf"], xr, g1, tm)
        xr = peer_layer(xr, p["norm_ffn_g"][l], sc2, sh2, g2, p["peer"][l], 512, 1024)
    y = final_norm(xr, p["final_norm_g"], tm).reshape(bsz, t, D_MODEL)
    return y, out


def kernel(x_prompt, x_sample, c_prompt, c_sample, state_s5_re, state_s5_im, state_hgrn, cache_swa_k, cache_swa_v, state_gla, ada_w, ada_b, norm_mix_g, norm_ffn_g, final_norm_g, w_in_even, w_out_even, s5_lambda_re, s5_lambda_im, s5_log_dt, s5_b_re, s5_b_im, s5_c_re, s5_c_im, s5_d, s5_glu_w, s5_glu_b, hgrn_lb_logits, hgrn_norm_g, w_in_odd, w_out_odd, swa_sinks, gla_w_gate2, gla_b_gate, gla_norm_g, peer_wq, peer_keys, peer_u, peer_v):
    w = dict(ada_w=ada_w, ada_b=ada_b, norm_mix_g=norm_mix_g, norm_ffn_g=norm_ffn_g, final_norm_g=final_norm_g,
             w_in_even=w_in_even, w_out_even=w_out_even, s5_lambda_re=s5_lambda_re, s5_lambda_im=s5_lambda_im,
             s5_log_dt=s5_log_dt, s5_b_re=s5_b_re, s5_b_im=s5_b_im, s5_c_re=s5_c_re, s5_c_im=s5_c_im, s5_d=s5_d,
             s5_glu_w=s5_glu_w, s5_glu_b=s5_glu_b, hgrn_lb_logits=hgrn_lb_logits, hgrn_norm_g=hgrn_norm_g,
             w_in_odd=w_in_odd, w_out_odd=w_out_odd, swa_sinks=swa_sinks, gla_w_gate2=gla_w_gate2,
             gla_b_gate=gla_b_gate, gla_norm_g=gla_norm_g, peer_wq=peer_wq, peer_keys=peer_keys,
             peer_u=peer_u, peer_v=peer_v)
    p = _prepare_weights(w)
    bp, bs = x_prompt.shape[0], x_sample.shape[0]
    pad = (-(bp + bs)) % 8
    c_all = jnp.concatenate([c_prompt, c_sample, jnp.zeros((pad, D_MODEL), F32)], axis=0)
    mod = ada_mod(c_all, ada_w, ada_b)
    y_p, o_p = _trunk(x_prompt, mod[:, :bp], p, sample=False,
                      s5_h0=jnp.zeros((bp, 2 * S5_FLAT), F32),
                      hg_s0=jnp.zeros((bp, HG_HEADS, HG_HEAD_DIM, HG_HEAD_DIM), F32),
                      k_buf=None, v_buf=None, gla_s0=jnp.zeros((bp, GLA_HEADS, GLA_DK, GLA_DV), F32))
    s5_h0 = jnp.concatenate([state_s5_re[0].reshape(bs, S5_FLAT), state_s5_im[0].reshape(bs, S5_FLAT)], axis=1)
    y_s, o_s = _trunk(x_sample, mod[:, bp:bp + bs], p, sample=True, s5_h0=s5_h0, hg_s0=state_hgrn[0],
                      k_buf=cache_swa_k[0], v_buf=cache_swa_v[0], gla_s0=state_gla[0])
    names = ("s5_re", "s5_im", "hgrn", "swa_k", "swa_v", "gla")
    return (y_p, y_s) + tuple(o_p[n] for n in names) + tuple(o_s[n] for n in names)
```

```python
import functools
import math

import jax
import jax.numpy as jnp
import numpy as np
from jax import lax
from jax.experimental import pallas as pl
from jax.experimental.pallas import tpu as pltpu

F32 = jnp.float32
BF16 = jnp.bfloat16

D_MODEL = 1024
DEPTH = 2
HALF = D_MODEL // 2
EPS = 1e-6
S5_GROUPS = 32
S5_GROUP_CH = 16
S5_STATE = 64
S5_FLAT = S5_GROUPS * S5_STATE
HG_HEADS = 4
HG_HEAD_DIM = 128
GLA_HEADS = 4
GLA_DK = 64
GLA_DV = 128
GLA_RANK = 16
GLA_TAU = 16.0
SWA_HEADS = 8
SWA_KV_HEADS = 2
SWA_HEAD_DIM = 64
SWA_REP = SWA_HEADS // SWA_KV_HEADS
WINDOW = 128
PEER_HEADS = 8
PEER_TOPK = 16
N_KEYS = 128
N_EXPERTS = N_KEYS * N_KEYS
PEER_QHALF = 128

LANES = 128
SAMPLE_PAD = 16
VMEM_LIMIT = 56 * 1024 * 1024

NT_DIMS = (((1,), (1,)), ((), ()))
TN_DIMS = (((0,), (0,)), ((), ()))


def _params(*sem):
    return pltpu.CompilerParams(dimension_semantics=sem, vmem_limit_bytes=VMEM_LIMIT)


def _dot(a, b):
    return jnp.dot(a, b, preferred_element_type=F32)


def _gelu(x):
    return 0.5 * x * (1.0 + jnp.tanh(math.sqrt(2.0 / math.pi) * (x + 0.044715 * (x * x * x))))


def _sigmoid(x):
    return 1.0 / (1.0 + jnp.exp(-x))


def _log_sigmoid(x):
    return jnp.minimum(x, 0.0) - jnp.log(1.0 + jnp.exp(-jnp.abs(x)))


def _norm_mod(x, g, sc, sh):
    r = lax.rsqrt(jnp.mean(x * x, axis=-1, keepdims=True) + EPS)
    return x * r * g * (1.0 + sc) + sh


def _split3(x):
    hi = x.astype(BF16)
    r1 = x - hi.astype(F32)
    mid = r1.astype(BF16)
    lo = (r1 - mid.astype(F32)).astype(BF16)
    return hi, mid, lo


def _ada_kernel(c_ref, w_ref, b_ref, o_ref):
    c = c_ref[...]
    a = c * _sigmoid(c)
    o_ref[...] = _dot(a.astype(BF16), w_ref[...].astype(BF16)) + b_ref[...]


def ada_mod(c, ada_w, ada_b):
    rows = c.shape[0]
    tn = 1536
    n = ada_w.shape[-1]
    return pl.pallas_call(
        _ada_kernel,
        grid=(DEPTH, n // tn),
        in_specs=[pl.BlockSpec((rows, D_MODEL), lambda l, j: (0, 0)),
                  pl.BlockSpec((None, D_MODEL, tn), lambda l, j: (l, 0, j)),
                  pl.BlockSpec((None, 1, tn), lambda l, j: (l, 0, j))],
        out_specs=pl.BlockSpec((None, rows, tn), lambda l, j: (l, 0, j)),
        out_shape=jax.ShapeDtypeStruct((DEPTH, rows, n), F32),
        compiler_params=_params("parallel", "parallel"),
        name="ada_mod",
    )(c, ada_w, ada_b.reshape(DEPTH, 1, n))


def _norm_matmul_kernel(x_ref, g_ref, sc_ref, sh_ref, w_ref, o_ref):
    h = _norm_mod(x_ref[...], g_ref[...], sc_ref[...], sh_ref[...])
    o_ref[...] = _dot(h.astype(BF16), w_ref[...])


def _mod_spec(mod, tm):
    if mod.shape[1] == 1:
        return pl.BlockSpec((None, 1, D_MODEL), lambda b, i: (b, 0, 0))
    return pl.BlockSpec((None, tm, D_MODEL), lambda b, i: (b, i, 0))


def norm_matmul(x, g, sc, sh, w, tm):
    bsz, t, _ = x.shape
    n = w.shape[1]
    return pl.pallas_call(
        _norm_matmul_kernel,
        grid=(bsz, t // tm),
        in_specs=[pl.BlockSpec((None, tm, D_MODEL), lambda b, i: (b, i, 0)),
                  pl.BlockSpec((1, D_MODEL), lambda b, i: (0, 0)),
                  _mod_spec(sc, tm), _mod_spec(sh, tm),
                  pl.BlockSpec((D_MODEL, n), lambda b, i: (0, 0))],
        out_specs=pl.BlockSpec((None, tm, n), lambda b, i: (b, i, 0)),
        out_shape=jax.ShapeDtypeStruct((bsz, t, n), F32),
        compiler_params=_params("parallel", "parallel"),
        name="norm_matmul",
    )(x, g.reshape(1, D_MODEL), sc, sh, w)


def _out_proj_kernel(a1_ref, a2_ref, w_ref, x_ref, gate_ref, o_ref):
    k1 = a1_ref.shape[-1]
    mix = _dot(a1_ref[...].astype(BF16), w_ref[:k1, :]) + _dot(a2_ref[...].astype(BF16), w_ref[k1:, :])
    o_ref[...] = x_ref[...] + gate_ref[...] * mix


def out_proj(a1, a2, w, x, gate, tm):
    bsz, t, _ = x.shape
    k1, k2 = a1.shape[-1], a2.shape[-1]
    return pl.pallas_call(
        _out_proj_kernel,
        grid=(bsz, t // tm),
        in_specs=[pl.BlockSpec((None, tm, k1), lambda b, i: (b, i, 0)),
                  pl.BlockSpec((None, tm, k2), lambda b, i: (b, i, 0)),
                  pl.BlockSpec((k1 + k2, D_MODEL), lambda b, i: (0, 0)),
                  pl.BlockSpec((None, tm, D_MODEL), lambda b, i: (b, i, 0)),
                  _mod_spec(gate, tm)],
        out_specs=pl.BlockSpec((None, tm, D_MODEL), lambda b, i: (b, i, 0)),
        out_shape=jax.ShapeDtypeStruct((bsz, t, D_MODEL), F32),
        compiler_params=_params("parallel", "parallel"),
        name="out_proj",
    )(a1, a2, w, x, gate)


def _final_norm_kernel(x_ref, g_ref, o_ref):
    x = x_ref[...]
    o_ref[...] = x * lax.rsqrt(jnp.mean(x * x, axis=-1, keepdims=True) + EPS) * g_ref[...]


def final_norm(x, g, tm):
    bsz, t, _ = x.shape
    return pl.pallas_call(
        _final_norm_kernel,
        grid=(bsz, t // tm),
        in_specs=[pl.BlockSpec((None, tm, D_MODEL), lambda b, i: (b, i, 0)),
                  pl.BlockSpec((1, D_MODEL), lambda b, i: (0, 0))],
        out_specs=pl.BlockSpec((None, tm, D_MODEL), lambda b, i: (b, i, 0)),
        out_shape=jax.ShapeDtypeStruct((bsz, t, D_MODEL), F32),
        compiler_params=_params("parallel", "parallel"),
        name="final_norm",
    )(x, g.reshape(1, D_MODEL))


def _cmul_add(ar, ai, hr, hi, br, bi):
    return ar * hr - ai * hi + br, ar * hi + ai * hr + bi


def _s5_kernel(u_ref, bblk_ref, cblk_ref, lam_ref, d_ref, gw_ref, gb_ref, h0_ref,
               y_ref, hlast_ref, h_scr, hp_scr, carry_scr, *, stride, chained):
    nc = u_ref.shape[1]
    p2 = S5_FLAT
    for tau in range(stride):
        h_scr[tau] = _dot(u_ref[tau].astype(BF16), bblk_ref[...])
    l1r, l1i = lam_ref[0:1, :p2], lam_ref[0:1, p2:]

    if chained:
        i = pl.program_id(1)

        @pl.when(i == 0)
        def _():
            carry_scr[...] = h0_ref[...]

        hr, hi = h_scr[0, :, :p2], h_scr[0, :, p2:]
        for tau in range(1, stride):
            hr, hi = _cmul_add(l1r, l1i, hr, hi, h_scr[tau, :, :p2], h_scr[tau, :, p2:])
            h_scr[tau, :, :p2] = hr
            h_scr[tau, :, p2:] = hi
        llr, lli = lam_ref[stride - 1:stride, :p2], lam_ref[stride - 1:stride, p2:]

        cr, ci = carry_scr[:, :p2], carry_scr[:, p2:]
        for c in range(nc):
            hp_scr[c:c + 1, :p2] = cr
            hp_scr[c:c + 1, p2:] = ci
            cr, ci = _cmul_add(llr, lli, cr, ci, h_scr[stride - 1, c:c + 1, :p2], h_scr[stride - 1, c:c + 1, p2:])
        carry_scr[:, :p2] = cr
        carry_scr[:, p2:] = ci
        hpr, hpi = hp_scr[:, :p2], hp_scr[:, p2:]
        for tau in range(stride):
            pr, pi = lam_ref[tau:tau + 1, :p2], lam_ref[tau:tau + 1, p2:]
            hr, hi = _cmul_add(pr, pi, hpr, hpi, h_scr[tau, :, :p2], h_scr[tau, :, p2:])
            h_scr[tau, :, :p2] = hr
            h_scr[tau, :, p2:] = hi
        hlast_ref[:, :p2] = cr
        hlast_ref[:, p2:] = ci
    else:
        hr, hi = h0_ref[:, :p2], h0_ref[:, p2:]
        for tau in range(stride):
            hr, hi = _cmul_add(l1r, l1i, hr, hi, h_scr[tau, :, :p2], h_scr[tau, :, p2:])
            h_scr[tau, :, :p2] = hr
            h_scr[tau, :, p2:] = hi
        hlast_ref[:, :p2] = hr
        hlast_ref[:, p2:] = hi

    for tau in range(stride):
        y = _dot(h_scr[tau].astype(BF16), cblk_ref[...]) + d_ref[...] * u_ref[tau]
        g = _gelu(y)
        y_ref[tau] = g * _sigmoid(_dot(g.astype(BF16), gw_ref[...]) + gb_ref[...])


def s5_layer(u, prm, h0, *, stride, chained, nc):
    bsz, t, _ = u.shape
    chunks = t // stride
    u_de = u.reshape(bsz, chunks, stride, HALF).transpose(0, 2, 1, 3)
    hrows = 1 if chained else nc
    const = lambda b, i: (0, 0)
    kern = functools.partial(_s5_kernel, stride=stride, chained=chained)
    y_de, hlast = pl.pallas_call(
        kern,
        grid=(bsz, chunks // nc),
        in_specs=[pl.BlockSpec((None, stride, nc, HALF), lambda b, i: (b, 0, i, 0)),
                  pl.BlockSpec((HALF, 2 * S5_FLAT), const),
                  pl.BlockSpec((2 * S5_FLAT, HALF), const),
                  pl.BlockSpec((stride, 2 * S5_FLAT), const),
                  pl.BlockSpec((1, HALF), const),
                  pl.BlockSpec((HALF, HALF), const),
                  pl.BlockSpec((1, HALF), const),
                  pl.BlockSpec((None, hrows, 2 * S5_FLAT), lambda b, i: (b, 0, 0))],
        out_specs=[pl.BlockSpec((None, stride, nc, HALF), lambda b, i: (b, 0, i, 0)),
                   pl.BlockSpec((None, hrows, 2 * S5_FLAT), lambda b, i: (b, 0, 0))],
        out_shape=[jax.ShapeDtypeStruct((bsz, stride, chunks, HALF), F32),
                   jax.ShapeDtypeStruct((bsz, hrows, 2 * S5_FLAT), F32)],
        scratch_shapes=[pltpu.VMEM((stride, nc, 2 * S5_FLAT), F32),
                        pltpu.VMEM((nc, 2 * S5_FLAT), F32),
                        pltpu.VMEM((1, 2 * S5_FLAT), F32)],
        compiler_params=_params("parallel", "arbitrary"),
        name="s5_chained" if chained else "s5_batched",
    )(u_de, prm["bblk"], prm["cblk"], prm["lam_pow"][:stride], prm["d"], prm["glu_w"], prm["glu_b"], h0)
    return y_de.transpose(0, 2, 1, 3).reshape(bsz, t, HALF), hlast


def s5_params(lam_re, lam_im, log_dt, b_re, b_im, c_re, c_im, d_skip, glu_w, glu_b, max_stride):
    dt = jnp.exp(log_dt)[:, None]
    n = jnp.arange(1, max_stride + 1, dtype=F32)[:, None, None]
    mag = jnp.exp(n * (lam_re * dt)[None])
    ang = n * (lam_im * dt)[None]
    pow_re, pow_im = mag * jnp.cos(ang), mag * jnp.sin(ang)
    lam_pow = jnp.concatenate([pow_re.reshape(max_stride, S5_FLAT), pow_im.reshape(max_stride, S5_FLAT)], axis=1)
    den = lam_re * lam_re + lam_im * lam_im
    nr, ni = pow_re[0] - 1.0, pow_im[0]
    cr = (nr * lam_re + ni * lam_im) / den
    ci = (ni * lam_re - nr * lam_im) / den
    bb_re = cr[..., None] * b_re - ci[..., None] * b_im
    bb_im = cr[..., None] * b_im + ci[..., None] * b_re
    eye = jnp.eye(S5_GROUPS, dtype=F32)
    pack_b = lambda m: jnp.einsum("gpc,gh->gchp", m, eye).reshape(HALF, S5_FLAT)
    pack_c = lambda m: jnp.einsum("gcp,gh->gphc", m, eye).reshape(S5_FLAT, HALF)
    return dict(
        bblk=jnp.concatenate([pack_b(bb_re), pack_b(bb_im)], axis=1).astype(BF16),
        cblk=jnp.concatenate([pack_c(c_re), -pack_c(c_im)], axis=0).astype(BF16),
        lam_pow=lam_pow, d=d_skip.reshape(1, HALF), glu_w=glu_w.astype(BF16), glu_b=glu_b.reshape(1, HALF))


def _seg_matrices(length):
    t = np.arange(length)[:, None]
    r = np.arange(length)[None, :]
    pre, suf = [], []
    m = 2
    while m <= length:
        same = (t // m) == (r // m)
        pre.append(same & (r <= t))
        suf.append(same & (r > t))
        m *= 2
    return np.concatenate(pre + suf, axis=0).astype(np.float32)


def _pair_level(length):
    t = np.arange(length)[:, None]
    s = np.arange(length)[None, :]
    x = np.where(t > s, t ^ s, 0)
    lvl = np.where(x > 0, np.floor(np.log2(np.maximum(x, 1))), -1)
    return lvl.astype(np.int32)


def _glr_kernel(*refs, mode, heads, dk, dv, length, t_valid):
    if mode == "hgrn":
        (q_ref, f_ref, v_ref, gate_ref, lb_ref, ng_ref, seg_ref, lvl_ref, s0_ref,
         o_ref, slast_ref, s_scr) = refs
    else:
        (q_ref, k_ref, v_ref, gate_ref, glr_ref, w2_ref, b2_ref, ng_ref, seg_ref, lvl_ref, s0_ref,
         o_ref, slast_ref, s_scr) = refs
    i = pl.program_id(1)

    @pl.when(i == 0)
    def _():
        s_scr[...] = s0_ref[...]

    if mode == "hgrn":
        lb = lb_ref[...]
        hf = f_ref[...]
        q = q_ref[...]
        k = (1.0 - lb) * _sigmoid(-hf)
        g = jnp.log(lb + (1.0 - lb) * _sigmoid(hf))
        gate = _sigmoid(gate_ref[...])
    else:
        q = q_ref[...] * (dk ** -0.5)
        k = k_ref[...]
        g = _log_sigmoid(_dot(glr_ref[...].astype(BF16), w2_ref[...]) + b2_ref[...]) * (1.0 / GLA_TAU)
        og = gate_ref[...]
        gate = og * _sigmoid(og)
    if t_valid < length:
        live = lax.broadcasted_iota(jnp.int32, (length, 1), 0) < t_valid
        g = jnp.where(live, g, 0.0)
        k = jnp.where(live, k, 0.0)
    v = v_ref[...]

    n_lv = int(math.log2(length))
    seg = seg_ref[...]
    pieces = _split3(g)
    ps = _dot(seg, pieces[0]) + _dot(seg, pieces[1]) + _dot(seg, pieces[2])
    lvl = lvl_ref[...]
    ones = jnp.ones((length, dv), BF16)

    def prefix(j):
        return ps[j * length:(j + 1) * length]

    def suffix(j):
        return ps[(n_lv + j) * length:(n_lv + j + 1) * length]

    for h in range(heads):
        cs = slice(h * dk, (h + 1) * dk)
        vs = slice(h * dv, (h + 1) * dv)
        qh, kh, gh, vh = q[:, cs], k[:, cs], g[:, cs], v[:, vs]
        vb = vh.astype(BF16)
        scores = jnp.zeros((length, length), F32)
        for j in range(n_lv):
            if j == 0:
                qe, ke = qh * jnp.exp(gh), kh
            else:
                qe, ke = qh * jnp.exp(prefix(j - 1)[:, cs]), kh * jnp.exp(suffix(j - 1)[:, cs])
            sc = lax.dot_general(qe.astype(BF16), ke.astype(BF16), NT_DIMS, preferred_element_type=F32)
            scores = scores + jnp.where(lvl == j, sc, 0.0)
        diag = jnp.sum(qh * kh, axis=-1, keepdims=True)
        s_old = s_scr[h]
        q_in = qh * jnp.exp(prefix(n_lv - 1)[:, cs])
        o = _dot(scores.astype(BF16), vb) + diag * vh + _dot(q_in.astype(BF16), s_old.astype(BF16))
        k_out = (kh * jnp.exp(suffix(n_lv - 1)[:, cs])).astype(BF16)
        total = sum(lax.dot_general(p[:, cs], ones, TN_DIMS, preferred_element_type=F32) for p in pieces)
        s_new = jnp.exp(total) * s_old + lax.dot_general(k_out, vb, TN_DIMS, preferred_element_type=F32)
        s_scr[h] = s_new
        slast_ref[h] = s_new
        o = o * lax.rsqrt(jnp.mean(o * o, axis=-1, keepdims=True) + EPS) * ng_ref[:, vs] * gate[:, vs]
        o_ref[:, vs] = o


def gated_linear(z, mode, prm, s0, *, length, t_valid):
    bsz, t, _ = z.shape
    heads = HG_HEADS
    dk, dv = (HG_HEAD_DIM, HG_HEAD_DIM) if mode == "hgrn" else (GLA_DK, GLA_DV)
    const = lambda b, i: (0, 0)
    col = lambda width, j: pl.BlockSpec((None, length, width), lambda b, i: (b, i, j))
    seg = jnp.asarray(_seg_matrices(length), BF16)
    lvl = jnp.asarray(_pair_level(length))
    if mode == "hgrn":
        ins = [z, z, z, z, prm["lb"], prm["norm_g"]]
        specs = [col(HALF, 1), col(HALF, 2), col(HALF, 3), col(HALF, 4),
                 pl.BlockSpec((1, HALF), const), pl.BlockSpec((1, HALF), const)]
    else:
        ins = [z, z, z, z, z, prm["w2"], prm["b2"], prm["norm_g"]]
        specs = [col(HALF // 2, 6), col(HALF // 2, 7), col(HALF, 1), col(HALF, 2), col(LANES, 18),
                 pl.BlockSpec((LANES, HALF // 2), const), pl.BlockSpec((1, HALF // 2), const),
                 pl.BlockSpec((1, HALF), const)]
    ins += [seg, lvl, s0]
    specs += [pl.BlockSpec(seg.shape, const), pl.BlockSpec(lvl.shape, const),
              pl.BlockSpec((None, heads, dk, dv), lambda b, i: (b, 0, 0, 0))]
    kern = functools.partial(_glr_kernel, mode=mode, heads=heads, dk=dk, dv=dv, length=length, t_valid=t_valid)
    return pl.pallas_call(
        kern,
        grid=(bsz, t // length),
        in_specs=specs,
        out_specs=[pl.BlockSpec((None, length, heads * dv), lambda b, i: (b, i, 0)),
                   pl.BlockSpec((None, heads, dk, dv), lambda b, i: (b, 0, 0, 0))],
        out_shape=[jax.ShapeDtypeStruct((bsz, t, heads * dv), F32),
                   jax.ShapeDtypeStruct((bsz, heads, dk, dv), F32)],
        scratch_shapes=[pltpu.VMEM((heads, dk, dv), F32)],
        compiler_params=_params("parallel", "arbitrary"),
        name="gated_linear_" + mode,
    )(*ins)


def _swa_kernel(q_ref, kc_ref, vc_ref, kp_ref, vp_ref, sink_ref, o_ref, *, lq, first_block_has_no_prev):
    hd = SWA_HEAD_DIM
    rows = SWA_REP * lq
    jp = lax.broadcasted_iota(jnp.int32, (rows, WINDOW), 0) & (lq - 1)
    if first_block_has_no_prev:
        jp = jp + jnp.where(pl.program_id(1) > 0, 0, WINDOW)
    ok_prev = lax.broadcasted_iota(jnp.int32, (rows, WINDOW), 1) > jp
    jc = lax.broadcasted_iota(jnp.int32, (rows, lq), 0) & (lq - 1)
    ok_cur = lax.broadcasted_iota(jnp.int32, (rows, lq), 1) <= jc
    head_of_row = lax.broadcasted_iota(jnp.int32, (rows, 1), 0) >> int(math.log2(lq))
    q = q_ref[...]
    scale = hd ** -0.5
    for g in range(SWA_KV_HEADS):
        ks = slice(g * hd, (g + 1) * hd)
        qs = jnp.concatenate([q[:, (g * SWA_REP + rr) * hd:(g * SWA_REP + rr + 1) * hd]
                              for rr in range(SWA_REP)], axis=0).astype(BF16)
        sp = lax.dot_general(qs, kp_ref[:, ks].astype(BF16), NT_DIMS, preferred_element_type=F32) * scale
        sc = lax.dot_general(qs, kc_ref[:, ks].astype(BF16), NT_DIMS, preferred_element_type=F32) * scale
        sp = jnp.where(ok_prev, sp, -jnp.inf)
        sc = jnp.where(ok_cur, sc, -jnp.inf)
        sink = jnp.zeros((rows, 1), F32)
        for rr in range(SWA_REP):
            hsel = g * SWA_REP + rr
            sink = jnp.where(head_of_row == rr, sink_ref[hsel:hsel + 1, 0:1], sink)
        m = jnp.maximum(jnp.maximum(jnp.max(sp, axis=-1, keepdims=True), jnp.max(sc, axis=-1, keepdims=True)), sink)
        pp = jnp.exp(sp - m)
        pc = jnp.exp(sc - m)
        den = jnp.sum(pp, axis=-1, keepdims=True) + jnp.sum(pc, axis=-1, keepdims=True) + jnp.exp(sink - m)
        o = (_dot(pp.astype(BF16), vp_ref[:, ks].astype(BF16)) + _dot(pc.astype(BF16), vc_ref[:, ks].astype(BF16))) / den
        for rr in range(SWA_REP):
            hcol = (g * SWA_REP + rr) * hd
            o_ref[:, hcol:hcol + hd] = o[rr * lq:(rr + 1) * lq]


def swa_attention(z, k_prev, v_prev, sinks, *, lq, prompt):
    bsz, t, _ = z.shape
    kvw = SWA_KV_HEADS * SWA_HEAD_DIM
    cur = lambda j: pl.BlockSpec((None, lq, kvw), lambda b, i: (b, i, j))
    if prompt:
        prev = lambda j: pl.BlockSpec((None, lq, kvw), lambda b, i: (b, jnp.maximum(i - 1, 0), j))
        kp, vp, prev_specs = z, z, [prev(16), prev(17)]
    else:
        kp, vp = k_prev, v_prev
        prev_specs = [pl.BlockSpec((None, WINDOW, kvw), lambda b, i: (b, 0, 0))] * 2
    sink_arr = jnp.broadcast_to(sinks.reshape(SWA_HEADS, 1), (SWA_HEADS, LANES))
    kern = functools.partial(_swa_kernel, lq=lq, first_block_has_no_prev=prompt)
    return pl.pallas_call(
        kern,
        grid=(bsz, t // lq),
        in_specs=[pl.BlockSpec((None, lq, HALF), lambda b, i: (b, i, 0)), cur(16), cur(17)] + prev_specs
                 + [pl.BlockSpec((SWA_HEADS, LANES), lambda b, i: (0, 0))],
        out_specs=pl.BlockSpec((None, lq, HALF), lambda b, i: (b, i, 0)),
        out_shape=jax.ShapeDtypeStruct((bsz, t, HALF), F32),
        compiler_params=_params("parallel", "parallel"),
        name="swa_prompt" if prompt else "swa_sample",
    )(z, z, z, kp, vp, sink_arr)


def _top16(s):
    key_iota = lax.broadcasted_iota(jnp.int32, s.shape, 0).astype(F32)
    row_iota = lax.broadcasted_iota(jnp.int32, (PEER_TOPK, s.shape[1]), 0)

    def body(a, carry):
        cur, rank, vals = carry
        m = jnp.max(cur, axis=0, keepdims=True)
        idx = jnp.min(jnp.where(cur == m, key_iota, float(N_KEYS)), axis=0, keepdims=True)
        hit = key_iota == idx
        rank = jnp.where(hit, a.astype(F32), rank)
        vals = jnp.where(row_iota == a, m, vals)
        return jnp.where(hit, -jnp.inf, cur), rank, vals

    init = (s, jnp.full(s.shape, float(PEER_TOPK), F32), jnp.zeros((PEER_TOPK, s.shape[1]), F32))
    _, rank, vals = lax.fori_loop(0, PEER_TOPK, body, init)
    return rank, vals


def _top16_tie_free(s):
    row_iota = lax.broadcasted_iota(jnp.int32, (PEER_TOPK, s.shape[1]), 0)

    def body(a, carry):
        cur, rank, vals = carry
        m = jnp.max(cur, axis=0, keepdims=True)
        hit = cur == m
        rank = jnp.where(hit, a.astype(F32), rank)
        vals = jnp.where(row_iota == a, m, vals)
        return jnp.where(hit, -jnp.inf, cur), rank, vals

    init = (s, jnp.full(s.shape, float(PEER_TOPK), F32), jnp.zeros((PEER_TOPK, s.shape[1]), F32))
    _, rank, vals = lax.fori_loop(0, PEER_TOPK, body, init)
    ranked = jnp.sum(jnp.where(rank < float(PEER_TOPK), 1.0, 0.0), axis=0, keepdims=True)
    has_tie = jnp.max(ranked) > float(PEER_TOPK)
    return lax.cond(has_tie, lambda: _top16(s), lambda: (rank, vals))


def _merge16(v0, v1):
    lanes = v0.shape[1]
    a_iota = lax.broadcasted_iota(jnp.int32, (PEER_TOPK, lanes), 0).astype(F32)
    top = v0[0:1] + v1[0:1]

    def body(_, carry):
        front, count, z = carry
        m = jnp.max(front, axis=0, keepdims=True)
        a_star = jnp.min(jnp.where(front == m, a_iota, float(PEER_TOPK)), axis=0, keepdims=True)
        hit = a_iota == a_star
        count = jnp.where(hit, count + 1.0, count)
        cn = jnp.sum(jnp.where(hit, count, 0.0), axis=0, keepdims=True)
        nxt = jnp.sum(jnp.where(a_iota == cn, v1, 0.0), axis=0, keepdims=True)
        front = jnp.where(hit, jnp.where(cn < float(PEER_TOPK), v0 + nxt, -jnp.inf), front)
        return front, count, z + jnp.exp(m - top)

    init = (v0 + v1[0:1], jnp.zeros((PEER_TOPK, lanes), F32), jnp.zeros((1, lanes), F32))
    _, count, z = lax.fori_loop(0, PEER_TOPK, body, init)
    return count, z


def _peer_topk_kernel(x_ref, g_ref, sc_ref, sh_ref, wqt_ref, keys_ref,
                      ht_ref, rank1_ref, e1_ref, cnt0_ref, e0_ref, qt_scr, s_scr):
    tt = x_ref.shape[0]
    h = _norm_mod(x_ref[...], g_ref[...], sc_ref[...], sh_ref[...])
    htb = h.T.astype(BF16)
    ht_ref[...] = htb
    qt_scr[...] = _dot(wqt_ref[...], htb)
    for hh in range(PEER_HEADS):
        for c in range(2):
            r0 = (hh * 2 + c) * PEER_QHALF
            s_scr[c] = _dot(keys_ref[c], qt_scr[r0:r0 + PEER_QHALF, :].astype(BF16))
        for lg in range(tt // LANES):
            ls = slice(lg * LANES, (lg + 1) * LANES)
            s0, s1 = s_scr[0, :, ls], s_scr[1, :, ls]
            rank0, v0 = _top16_tie_free(s0)
            rank1, v1 = _top16_tie_free(s1)
            count, z = _merge16(v0, v1)
            cnt0 = jnp.zeros_like(s0)
            for a in range(PEER_TOPK):
                cnt0 = cnt0 + jnp.where(rank0 == float(a), count[a:a + 1], 0.0)
            rank1_ref[hh, :, ls] = rank1
            e1_ref[hh, :, ls] = jnp.exp(s1 - v1[0:1])
            cnt0_ref[hh, :, ls] = cnt0
            e0_ref[hh, :, ls] = jnp.exp(s0 - v0[0:1]) / z


def peer_topk(x, g, sc, sh, wqt, keys, tt):
    bsz, t, _ = x.shape
    n = bsz * t
    nq = wqt.shape[0]
    tok = lambda b, i: (0, 0, b * (t // tt) + i)
    stat = pl.BlockSpec((PEER_HEADS, N_KEYS, tt), tok)
    stat_shape = jax.ShapeDtypeStruct((PEER_HEADS, N_KEYS, n), F32)
    return pl.pallas_call(
        _peer_topk_kernel,
        grid=(bsz, t // tt),
        in_specs=[pl.BlockSpec((None, tt, D_MODEL), lambda b, i: (b, i, 0)),
                  pl.BlockSpec((1, D_MODEL), lambda b, i: (0, 0)),
                  _mod_spec(sc, tt), _mod_spec(sh, tt),
                  pl.BlockSpec((nq, D_MODEL), lambda b, i: (0, 0)),
                  pl.BlockSpec((2, N_KEYS, PEER_QHALF), lambda b, i: (0, 0, 0))],
        out_specs=[pl.BlockSpec((D_MODEL, tt), lambda b, i: (0, b * (t // tt) + i)), stat, stat, stat, stat],
        out_shape=[jax.ShapeDtypeStruct((D_MODEL, n), BF16), stat_shape, stat_shape, stat_shape, stat_shape],
        scratch_shapes=[pltpu.VMEM((nq, tt), F32), pltpu.VMEM((2, N_KEYS, tt), F32)],
        compiler_params=_params("parallel", "parallel"),
        name="peer_topk",
    )(x, g.reshape(1, D_MODEL), sc, sh, wqt, keys)


def _peer_dense_kernel(ht_ref, rank1_ref, e1_ref, cnt0_ref, e0_ref, u_ref, vt_ref, x_ref, gate_ref,
                       o_ref, acc_scr, a_scr, h_scr):
    e = pl.program_id(2)
    et, tt = a_scr.shape
    nb = et // N_KEYS

    @pl.when(e == 0)
    def _():
        acc_scr[...] = jnp.zeros_like(acc_scr)

    a_scr[...] = _dot(u_ref[...], ht_ref[...])

    for ib in range(nb):
        rs = slice(ib * N_KEYS, (ib + 1) * N_KEYS)
        for lg in range(tt // LANES):
            ls = slice(lg * LANES, (lg + 1) * LANES)
            w = jnp.zeros((N_KEYS, LANES), F32)
            for hh in range(PEER_HEADS):
                cnt = cnt0_ref[hh, ib:ib + 1, ls]
                e0 = e0_ref[hh, ib:ib + 1, ls]
                w = w + jnp.where(rank1_ref[hh, :, ls] < cnt, e1_ref[hh, :, ls] * e0, 0.0)
            h_scr[rs, ls] = (_gelu(a_scr[rs, ls]) * w).astype(BF16)
    acc_scr[...] += _dot(vt_ref[...], h_scr[...])

    @pl.when(e == pl.num_programs(2) - 1)
    def _():
        o_ref[...] = x_ref[...] + gate_ref[...] * acc_scr[...].T


def peer_dense(ht, stats, u_bf, vt_bf, x, gate, tt, et):
    bsz, t, _ = x.shape
    tok = lambda b, i, e: (0, 0, b * (t // tt) + i)
    stat = pl.BlockSpec((PEER_HEADS, N_KEYS, tt), tok)
    if gate.shape[1] == 1:
        gate_spec = pl.BlockSpec((None, 1, D_MODEL), lambda b, i, e: (b, 0, 0))
    else:
        gate_spec = pl.BlockSpec((None, tt, D_MODEL), lambda b, i, e: (b, i, 0))
    return pl.pallas_call(
        _peer_dense_kernel,
        grid=(bsz, t // tt, N_EXPERTS // et),
        in_specs=[pl.BlockSpec((D_MODEL, tt), lambda b, i, e: (0, b * (t // tt) + i)),
                  stat, stat,
                  pl.BlockSpec((PEER_HEADS, et // N_KEYS, tt), lambda b, i, e: (0, e, b * (t // tt) + i)),
                  pl.BlockSpec((PEER_HEADS, et // N_KEYS, tt), lambda b, i, e: (0, e, b * (t // tt) + i)),
                  pl.BlockSpec((et, D_MODEL), lambda b, i, e: (e, 0)),
                  pl.BlockSpec((D_MODEL, et), lambda b, i, e: (0, e)),
                  pl.BlockSpec((None, tt, D_MODEL), lambda b, i, e: (b, i, 0)),
                  gate_spec],
        out_specs=pl.BlockSpec((None, tt, D_MODEL), lambda b, i, e: (b, i, 0)),
        out_shape=jax.ShapeDtypeStruct((bsz, t, D_MODEL), F32),
        scratch_shapes=[pltpu.VMEM((D_MODEL, tt), F32), pltpu.VMEM((et, tt), F32), pltpu.VMEM((et, tt), BF16)],
        compiler_params=_params("parallel", "parallel", "arbitrary"),
        name="peer_dense",
    )(ht, *stats, u_bf, vt_bf, x, gate)


def peer_layer(x, g, sc, sh, gate, w, tt, et):
    ht, *stats = peer_topk(x, g, sc, sh, w["wqt"], w["keys"], tt)
    return peer_dense(ht, stats, w["u"], w["vt"], x, gate, tt, et)


def _odd_in_weight(w_in_odd):
    qa, ka, va, qd, kd, vd, glr, og = jnp.split(w_in_odd, [512, 640, 768, 1024, 1280, 1792, 1808], axis=1)
    glr = jnp.pad(glr, ((0, 0), (0, LANES - GLA_RANK)))
    return jnp.concatenate([qa, vd, og, qd, kd, ka, va, glr], axis=1).astype(BF16)


def _prepare_weights(w):
    p = dict(w)
    p["lb0"] = jax.nn.softmax(w["hgrn_lb_logits"], axis=0)[0].reshape(1, HALF)
    p["w_in_even_bf"] = w["w_in_even"][0].astype(BF16)
    p["w_out_even_bf"] = w["w_out_even"][0].astype(BF16)
    p["w_in_odd_bf"] = _odd_in_weight(w["w_in_odd"][0])
    p["w_out_odd_bf"] = w["w_out_odd"][0].astype(BF16)
    p["s5"] = s5_params(w["s5_lambda_re"][0], w["s5_lambda_im"][0], w["s5_log_dt"][0], w["s5_b_re"][0],
                        w["s5_b_im"][0], w["s5_c_re"][0], w["s5_c_im"][0], w["s5_d"][0], w["s5_glu_w"][0],
                        w["s5_glu_b"][0], 8)
    p["hgrn"] = dict(lb=p["lb0"], norm_g=w["hgrn_norm_g"][0].reshape(1, HALF))
    p["gla"] = dict(w2=jnp.pad(w["gla_w_gate2"][0], ((0, LANES - GLA_RANK), (0, 0))).astype(BF16),
                    b2=w["gla_b_gate"][0].reshape(1, HALF // 2), norm_g=w["gla_norm_g"][0].reshape(1, HALF))
    p["peer"] = [dict(wqt=w["peer_wq"][l].T.astype(BF16), keys=w["peer_keys"][l].astype(BF16),
                      u=w["peer_u"][l].astype(BF16), vt=w["peer_v"][l].T.astype(BF16)) for l in range(DEPTH)]
    return p


def _trunk(x, mod, p, *, sample, s5_h0, hg_s0, k_buf, v_buf, gla_s0):
    bsz, t, _ = x.shape
    rows = bsz * t
    if sample:
        xr = x.reshape(1, rows, D_MODEL)
        tm = rows
        expand = lambda m: jnp.repeat(m, t, axis=0).reshape(1, rows, D_MODEL)
    else:
        xr = x
        tm = 512
        expand = lambda m: m.reshape(bsz, 1, D_MODEL)
    out = {}
    for l in range(DEPTH):
        sh1, sc1, g1, sh2, sc2, g2 = [expand(m) for m in jnp.split(mod[l], 6, axis=-1)]
        if l % 2 == 0:
            z = norm_matmul(xr, p["norm_mix_g"][l], sc1, sh1, p["w_in_even_bf"], tm)
            if sample:
                ya, hl = s5_layer(z[..., :HALF], p["s5"], s5_h0.reshape(1, bsz, 2 * S5_FLAT), stride=t, chained=False, nc=bsz)
                zp = jnp.pad(z.reshape(bsz, t, -1), ((0, 0), (0, SAMPLE_PAD - t), (0, 0)))
                yb, hs = gated_linear(zp, "hgrn", p["hgrn"], hg_s0, length=SAMPLE_PAD,t_valid=t)
                yb = yb[:, :t].reshape(1, rows, HALF)
            else:
                ya, hl = s5_layer(z[..., :HALF], p["s5"], s5_h0.reshape(bsz, 1, 2 * S5_FLAT), stride=8, chained=True, nc=32)
                yb, hs = gated_linear(z, "hgrn", p["hgrn"], hg_s0, length=128, t_valid=128)
            hl = hl.reshape(bsz, 2 * S5_FLAT)
            out["s5_re"] = hl[:, :S5_FLAT].reshape(1, bsz, S5_GROUPS, S5_STATE)
            out["s5_im"] = hl[:, S5_FLAT:].reshape(1, bsz, S5_GROUPS, S5_STATE)
            out["hgrn"] = hs[None]
            xr = out_proj(ya, yb, p["w_out_even_bf"], xr, g1, tm)
        else:
            z = norm_matmul(xr, p["norm_mix_g"][l], sc1, sh1, p["w_in_odd_bf"], tm)
            kvw = SWA_KV_HEADS * SWA_HEAD_DIM
            if sample:
                z3 = z.reshape(bsz, t, -1)
                zp = jnp.pad(z3, ((0, 0), (0, SAMPLE_PAD - t), (0, 0)))
                kb = k_buf.reshape(bsz, WINDOW, kvw)
                vb = v_buf.reshape(bsz, WINDOW, kvw)
                attn = swa_attention(zp, kb, vb, p["swa_sinks"][0], lq=SAMPLE_PAD, prompt=False)[:, :t]
                attn = attn.reshape(1, rows, HALF)
                new_k = jnp.concatenate([kb, z3[:, :, 2048:2048 + kvw]], axis=1)[:, -WINDOW:]
                new_v = jnp.concatenate([vb, z3[:, :, 2048 + kvw:2048 + 2 * kvw]], axis=1)[:, -WINDOW:]
                yb, gs = gated_linear(zp, "gla", p["gla"], gla_s0, length=SAMPLE_PAD,t_valid=t)
                yb = yb[:, :t].reshape(1, rows, HALF)
            else:
                attn = swa_attention(z, None, None, p["swa_sinks"][0], lq=WINDOW, prompt=True)
                new_k = z[:, -WINDOW:, 2048:2048 + kvw]
                new_v = z[:, -WINDOW:, 2048 + kvw:2048 + 2 * kvw]
                yb, gs = gated_linear(z, "gla", p["gla"], gla_s0, length=128, t_valid=128)
            out["swa_k"] = new_k.reshape(1, bsz, WINDOW, SWA_KV_HEADS, SWA_HEAD_DIM)
            out["swa_v"] = new_v.reshape(1, bsz, WINDOW, SWA_KV_HEADS, SWA_HEAD_DIM)
            out["gla"] = gs[None]
            xr = out_proj(attn, yb, p["w_out_odd_bf"], xr, g1, tm)
        xr = peer_layer(xr, p["norm_ffn_g"][l], sc2, sh2, g2, p["peer"][l], 512, 1024)
    y = final_norm(xr, p["final_norm_g"], tm).reshape(bsz, t, D_MODEL)
    return y, out


def kernel(x_prompt, x_sample, c_prompt, c_sample, state_s5_re, state_s5_im, state_hgrn, cache_swa_k, cache_swa_v, state_gla, ada_w, ada_b, norm_mix_g, norm_ffn_g, final_norm_g, w_in_even, w_out_even, s5_lambda_re, s5_lambda_im, s5_log_dt, s5_b_re, s5_b_im, s5_c_re, s5_c_im, s5_d, s5_glu_w, s5_glu_b, hgrn_lb_logits, hgrn_norm_g, w_in_odd, w_out_odd, swa_sinks, gla_w_gate2, gla_b_gate, gla_norm_g, peer_wq, peer_keys, peer_u, peer_v):
    w = dict(ada_w=ada_w, ada_b=ada_b, norm_mix_g=norm_mix_g, norm_ffn_g=norm_ffn_g, final_norm_g=final_norm_g,
             w_in_even=w_in_even, w_out_even=w_out_even, s5_lambda_re=s5_lambda_re, s5_lambda_im=s5_lambda_im,
             s5_log_dt=s5_log_dt, s5_b_re=s5_b_re, s5_b_im=s5_b_im, s5_c_re=s5_c_re, s5_c_im=s5_c_im, s5_d=s5_d,
             s5_glu_w=s5_glu_w, s5_glu_b=s5_glu_b, hgrn_lb_logits=hgrn_lb_logits, hgrn_norm_g=hgrn_norm_g,
             w_in_odd=w_in_odd, w_out_odd=w_out_odd, swa_sinks=swa_sinks, gla_w_gate2=gla_w_gate2,
             gla_b_gate=gla_b_gate, gla_norm_g=gla_norm_g, peer_wq=peer_wq, peer_keys=peer_keys,
             peer_u=peer_u, peer_v=peer_v)
    p = _prepare_weights(w)
    bp, bs = x_prompt.shape[0], x_sample.shape[0]
    pad = (-(bp + bs)) % 8
    c_all = jnp.concatenate([c_prompt, c_sample, jnp.zeros((pad, D_MODEL), F32)], axis=0)
    mod = ada_mod(c_all, ada_w, ada_b)
    y_p, o_p = _trunk(x_prompt, mod[:, :bp], p, sample=False,
                      s5_h0=jnp.zeros((bp, 2 * S5_FLAT), F32),
                      hg_s0=jnp.zeros((bp, HG_HEADS, HG_HEAD_DIM, HG_HEAD_DIM), F32),
                      k_buf=None, v_buf=None, gla_s0=jnp.zeros((bp, GLA_HEADS, GLA_DK, GLA_DV), F32))
    s5_h0 = jnp.concatenate([state_s5_re[0].reshape(bs, S5_FLAT), state_s5_im[0].reshape(bs, S5_FLAT)], axis=1)
    y_s, o_s = _trunk(x_sample, mod[:, bp:bp + bs], p, sample=True, s5_h0=s5_h0, hg_s0=state_hgrn[0],
                      k_buf=cache_swa_k[0], v_buf=cache_swa_v[0], gla_s0=state_gla[0])
    names = ("s5_re", "s5_im", "hgrn", "swa_k", "swa_v", "gla")
    return (y_p, y_s) + tuple(o_p[n] for n in names) + tuple(o_s[n] for n in names)
```

```python
import functools
import math

import jax
import jax.numpy as jnp
import numpy as np
from jax import lax
from jax.experimental import pallas as pl
from jax.experimental.pallas import tpu as pltpu

F32 = jnp.float32
BF16 = jnp.bfloat16

D_MODEL = 1024
DEPTH = 2
HALF = D_MODEL // 2
EPS = 1e-6
S5_GROUPS = 32
S5_GROUP_CH = 16
S5_STATE = 64
S5_FLAT = S5_GROUPS * S5_STATE
HG_HEADS = 4
HG_HEAD_DIM = 128
GLA_HEADS = 4
GLA_DK = 64
GLA_DV = 128
GLA_RANK = 16
GLA_TAU = 16.0
SWA_HEADS = 8
SWA_KV_HEADS = 2
SWA_HEAD_DIM = 64
SWA_REP = SWA_HEADS // SWA_KV_HEADS
WINDOW = 128
PEER_HEADS = 8
PEER_TOPK = 16
N_KEYS = 128
N_EXPERTS = N_KEYS * N_KEYS
PEER_QHALF = 128

LANES = 128
SAMPLE_PAD = 16
VMEM_LIMIT = 56 * 1024 * 1024

NT_DIMS = (((1,), (1,)), ((), ()))
TN_DIMS = (((0,), (0,)), ((), ()))


def _params(*sem):
    return pltpu.CompilerParams(dimension_semantics=sem, vmem_limit_bytes=VMEM_LIMIT)


def _dot(a, b):
    return jnp.dot(a, b, preferred_element_type=F32)


def _gelu(x):
    return 0.5 * x * (1.0 + jnp.tanh(math.sqrt(2.0 / math.pi) * (x + 0.044715 * (x * x * x))))


def _sigmoid(x):
    return 1.0 / (1.0 + jnp.exp(-x))


def _log_sigmoid(x):
    return jnp.minimum(x, 0.0) - jnp.log(1.0 + jnp.exp(-jnp.abs(x)))


def _norm_mod(x, g, sc, sh):
    r = lax.rsqrt(jnp.mean(x * x, axis=-1, keepdims=True) + EPS)
    return x * r * g * (1.0 + sc) + sh


def _split3(x):
    hi = x.astype(BF16)
    r1 = x - hi.astype(F32)
    mid = r1.astype(BF16)
    lo = (r1 - mid.astype(F32)).astype(BF16)
    return hi, mid, lo


def _ada_kernel(c_ref, w_ref, b_ref, o_ref):
    c = c_ref[...]
    a = c * _sigmoid(c)
    o_ref[...] = _dot(a.astype(BF16), w_ref[...].astype(BF16)) + b_ref[...]


def ada_mod(c, ada_w, ada_b):
    rows = c.shape[0]
    tn = 1536
    n = ada_w.shape[-1]
    return pl.pallas_call(
        _ada_kernel,
        grid=(DEPTH, n // tn),
        in_specs=[pl.BlockSpec((rows, D_MODEL), lambda l, j: (0, 0)),
                  pl.BlockSpec((None, D_MODEL, tn), lambda l, j: (l, 0, j)),
                  pl.BlockSpec((None, 1, tn), lambda l, j: (l, 0, j))],
        out_specs=pl.BlockSpec((None, rows, tn), lambda l, j: (l, 0, j)),
        out_shape=jax.ShapeDtypeStruct((DEPTH, rows, n), F32),
        compiler_params=_params("parallel", "parallel"),
        name="ada_mod",
    )(c, ada_w, ada_b.reshape(DEPTH, 1, n))


def _norm_matmul_kernel(x_ref, g_ref, sc_ref, sh_ref, w_ref, o_ref):
    h = _norm_mod(x_ref[...], g_ref[...], sc_ref[...], sh_ref[...])
    o_ref[...] = _dot(h.astype(BF16), w_ref[...])


def _mod_spec(mod, tm):
    if mod.shape[1] == 1:
        return pl.BlockSpec((None, 1, D_MODEL), lambda b, i: (b, 0, 0))
    return pl.BlockSpec((None, tm, D_MODEL), lambda b, i: (b, i, 0))


def norm_matmul(x, g, sc, sh, w, tm):
    bsz, t, _ = x.shape
    n = w.shape[1]
    return pl.pallas_call(
        _norm_matmul_kernel,
        grid=(bsz, t // tm),
        in_specs=[pl.BlockSpec((None, tm, D_MODEL), lambda b, i: (b, i, 0)),
                  pl.BlockSpec((1, D_MODEL), lambda b, i: (0, 0)),
                  _mod_spec(sc, tm), _mod_spec(sh, tm),
                  pl.BlockSpec((D_MODEL, n), lambda b, i: (0, 0))],
        out_specs=pl.BlockSpec((None, tm, n), lambda b, i: (b, i, 0)),
        out_shape=jax.ShapeDtypeStruct((bsz, t, n), F32),
        compiler_params=_params("parallel", "parallel"),
        name="norm_matmul",
    )(x, g.reshape(1, D_MODEL), sc, sh, w)


def _out_proj_kernel(a1_ref, a2_ref, w_ref, x_ref, gate_ref, o_ref):
    k1 = a1_ref.shape[-1]
    mix = _dot(a1_ref[...].astype(BF16), w_ref[:k1, :]) + _dot(a2_ref[...].astype(BF16), w_ref[k1:, :])
    o_ref[...] = x_ref[...] + gate_ref[...] * mix


def out_proj(a1, a2, w, x, gate, tm):
    bsz, t, _ = x.shape
    k1, k2 = a1.shape[-1], a2.shape[-1]
    return pl.pallas_call(
        _out_proj_kernel,
        grid=(bsz, t // tm),
        in_specs=[pl.BlockSpec((None, tm, k1), lambda b, i: (b, i, 0)),
                  pl.BlockSpec((None, tm, k2), lambda b, i: (b, i, 0)),
                  pl.BlockSpec((k1 + k2, D_MODEL), lambda b, i: (0, 0)),
                  pl.BlockSpec((None, tm, D_MODEL), lambda b, i: (b, i, 0)),
                  _mod_spec(gate, tm)],
        out_specs=pl.BlockSpec((None, tm, D_MODEL), lambda b, i: (b, i, 0)),
        out_shape=jax.ShapeDtypeStruct((bsz, t, D_MODEL), F32),
        compiler_params=_params("parallel", "parallel"),
        name="out_proj",
    )(a1, a2, w, x, gate)


def _final_norm_kernel(x_ref, g_ref, o_ref):
    x = x_ref[...]
    o_ref[...] = x * lax.rsqrt(jnp.mean(x * x, axis=-1, keepdims=True) + EPS) * g_ref[...]


def final_norm(x, g, tm):
    bsz, t, _ = x.shape
    return pl.pallas_call(
        _final_norm_kernel,
        grid=(bsz, t // tm),
        in_specs=[pl.BlockSpec((None, tm, D_MODEL), lambda b, i: (b, i, 0)),
                  pl.BlockSpec((1, D_MODEL), lambda b, i: (0, 0))],
        out_specs=pl.BlockSpec((None, tm, D_MODEL), lambda b, i: (b, i, 0)),
        out_shape=jax.ShapeDtypeStruct((bsz, t, D_MODEL), F32),
        compiler_params=_params("parallel", "parallel"),
        name="final_norm",
    )(x, g.reshape(1, D_MODEL))


def _cmul_add(ar, ai, hr, hi, br, bi):
    return ar * hr - ai * hi + br, ar * hi + ai * hr + bi


def _s5_kernel(u_ref, bblk_ref, cblk_ref, lam_ref, d_ref, gw_ref, gb_ref, h0_ref,
               y_ref, hlast_ref, h_scr, hp_scr, carry_scr, *, stride, chained):
    nc = u_ref.shape[1]
    p2 = S5_FLAT
    for tau in range(stride):
        h_scr[tau] = _dot(u_ref[tau].astype(BF16), bblk_ref[...])
    l1r, l1i = lam_ref[0:1, :p2], lam_ref[0:1, p2:]

    if chained:
        i = pl.program_id(1)

        @pl.when(i == 0)
        def _():
            carry_scr[...] = h0_ref[...]

        hr, hi = h_scr[0, :, :p2], h_scr[0, :, p2:]
        for tau in range(1, stride):
            hr, hi = _cmul_add(l1r, l1i, hr, hi, h_scr[tau, :, :p2], h_scr[tau, :, p2:])
            h_scr[tau, :, :p2] = hr
            h_scr[tau, :, p2:] = hi
        llr, lli = lam_ref[stride - 1:stride, :p2], lam_ref[stride - 1:stride, p2:]

        cr, ci = carry_scr[:, :p2], carry_scr[:, p2:]
        for c in range(nc):
            hp_scr[c:c + 1, :p2] = cr
            hp_scr[c:c + 1, p2:] = ci
            cr, ci = _cmul_add(llr, lli, cr, ci, h_scr[stride - 1, c:c + 1, :p2], h_scr[stride - 1, c:c + 1, p2:])
        carry_scr[:, :p2] = cr
        carry_scr[:, p2:] = ci
        hpr, hpi = hp_scr[:, :p2], hp_scr[:, p2:]
        for tau in range(stride):
            pr, pi = lam_ref[tau:tau + 1, :p2], lam_ref[tau:tau + 1, p2:]
            hr, hi = _cmul_add(pr, pi, hpr, hpi, h_scr[tau, :, :p2], h_scr[tau, :, p2:])
            h_scr[tau, :, :p2] = hr
            h_scr[tau, :, p2:] = hi
        hlast_ref[:, :p2] = cr
        hlast_ref[:, p2:] = ci
    else:
        hr, hi = h0_ref[:, :p2], h0_ref[:, p2:]
        for tau in range(stride):
            hr, hi = _cmul_add(l1r, l1i, hr, hi, h_scr[tau, :, :p2], h_scr[tau, :, p2:])
            h_scr[tau, :, :p2] = hr
            h_scr[tau, :, p2:] = hi
        hlast_ref[:, :p2] = hr
        hlast_ref[:, p2:] = hi

    for tau in range(stride):
        y = _dot(h_scr[tau].astype(BF16), cblk_ref[...]) + d_ref[...] * u_ref[tau]
        g = _gelu(y)
        y_ref[tau] = g * _sigmoid(_dot(g.astype(BF16), gw_ref[...]) + gb_ref[...])


def s5_layer(u, prm, h0, *, stride, chained, nc):
    bsz, t, _ = u.shape
    chunks = t // stride
    u_de = u.reshape(bsz, chunks, stride, HALF).transpose(0, 2, 1, 3)
    hrows = 1 if chained else nc
    const = lambda b, i: (0, 0)
    kern = functools.partial(_s5_kernel, stride=stride, chained=chained)
    y_de, hlast = pl.pallas_call(
        kern,
        grid=(bsz, chunks // nc),
        in_specs=[pl.BlockSpec((None, stride, nc, HALF), lambda b, i: (b, 0, i, 0)),
                  pl.BlockSpec((HALF, 2 * S5_FLAT), const),
                  pl.BlockSpec((2 * S5_FLAT, HALF), const),
                  pl.BlockSpec((stride, 2 * S5_FLAT), const),
                  pl.BlockSpec((1, HALF), const),
                  pl.BlockSpec((HALF, HALF), const),
                  pl.BlockSpec((1, HALF), const),
                  pl.BlockSpec((None, hrows, 2 * S5_FLAT), lambda b, i: (b, 0, 0))],
        out_specs=[pl.BlockSpec((None, stride, nc, HALF), lambda b, i: (b, 0, i, 0)),
                   pl.BlockSpec((None, hrows, 2 * S5_FLAT), lambda b, i: (b, 0, 0))],
        out_shape=[jax.ShapeDtypeStruct((bsz, stride, chunks, HALF), F32),
                   jax.ShapeDtypeStruct((bsz, hrows, 2 * S5_FLAT), F32)],
        scratch_shapes=[pltpu.VMEM((stride, nc, 2 * S5_FLAT), F32),
                        pltpu.VMEM((nc, 2 * S5_FLAT), F32),
                        pltpu.VMEM((1, 2 * S5_FLAT), F32)],
        compiler_params=_params("parallel", "arbitrary"),
        name="s5_chained" if chained else "s5_batched",
    )(u_de, prm["bblk"], prm["cblk"], prm["lam_pow"][:stride], prm["d"], prm["glu_w"], prm["glu_b"], h0)
    return y_de.transpose(0, 2, 1, 3).reshape(bsz, t, HALF), hlast


def s5_params(lam_re, lam_im, log_dt, b_re, b_im, c_re, c_im, d_skip, glu_w, glu_b, max_stride):
    dt = jnp.exp(log_dt)[:, None]
    n = jnp.arange(1, max_stride + 1, dtype=F32)[:, None, None]
    mag = jnp.exp(n * (lam_re * dt)[None])
    ang = n * (lam_im * dt)[None]
    pow_re, pow_im = mag * jnp.cos(ang), mag * jnp.sin(ang)
    lam_pow = jnp.concatenate([pow_re.reshape(max_stride, S5_FLAT), pow_im.reshape(max_stride, S5_FLAT)], axis=1)
    den = lam_re * lam_re + lam_im * lam_im
    nr, ni = pow_re[0] - 1.0, pow_im[0]
    cr = (nr * lam_re + ni * lam_im) / den
    ci = (ni * lam_re - nr * lam_im) / den
    bb_re = cr[..., None] * b_re - ci[..., None] * b_im
    bb_im = cr[..., None] * b_im + ci[..., None] * b_re
    eye = jnp.eye(S5_GROUPS, dtype=F32)
    pack_b = lambda m: jnp.einsum("gpc,gh->gchp", m, eye).reshape(HALF, S5_FLAT)
    pack_c = lambda m: jnp.einsum("gcp,gh->gphc", m, eye).reshape(S5_FLAT, HALF)
    return dict(
        bblk=jnp.concatenate([pack_b(bb_re), pack_b(bb_im)], axis=1).astype(BF16),
        cblk=jnp.concatenate([pack_c(c_re), -pack_c(c_im)], axis=0).astype(BF16),
        lam_pow=lam_pow, d=d_skip.reshape(1, HALF), glu_w=glu_w.astype(BF16), glu_b=glu_b.reshape(1, HALF))


def _seg_matrices(length):
    t = np.arange(length)[:, None]
    r = np.arange(length)[None, :]
    pre, suf = [], []
    m = 2
    while m <= length:
        same = (t // m) == (r // m)
        pre.append(same & (r <= t))
        suf.append(same & (r > t))
        m *= 2
    return np.concatenate(pre + suf, axis=0).astype(np.float32)


def _pair_level(length):
    t = np.arange(length)[:, None]
    s = np.arange(length)[None, :]
    x = np.where(t > s, t ^ s, 0)
    lvl = np.where(x > 0, np.floor(np.log2(np.maximum(x, 1))), -1)
    return lvl.astype(np.int32)


def _glr_kernel(*refs, mode, heads, dk, dv, length, t_valid):
    if mode == "hgrn":
        (q_ref, f_ref, v_ref, gate_ref, lb_ref, ng_ref, seg_ref, lvl_ref, s0_ref,
         o_ref, slast_ref, s_scr) = refs
    else:
        (q_ref, k_ref, v_ref, gate_ref, glr_ref, w2_ref, b2_ref, ng_ref, seg_ref, lvl_ref, s0_ref,
         o_ref, slast_ref, s_scr) = refs
    i = pl.program_id(1)

    @pl.when(i == 0)
    def _():
        s_scr[...] = s0_ref[...]

    if mode == "hgrn":
        lb = lb_ref[...]
        hf = f_ref[...]
        q = q_ref[...]
        k = (1.0 - lb) * _sigmoid(-hf)
        g = jnp.log(lb + (1.0 - lb) * _sigmoid(hf))
        gate = _sigmoid(gate_ref[...])
    else:
        q = q_ref[...] * (dk ** -0.5)
        k = k_ref[...]
        g = _log_sigmoid(_dot(glr_ref[...].astype(BF16), w2_ref[...]) + b2_ref[...]) * (1.0 / GLA_TAU)
        og = gate_ref[...]
        gate = og * _sigmoid(og)
    if t_valid < length:
        live = lax.broadcasted_iota(jnp.int32, (length, 1), 0) < t_valid
        g = jnp.where(live, g, 0.0)
        k = jnp.where(live, k, 0.0)
    v = v_ref[...]

    n_lv = int(math.log2(length))
    seg = seg_ref[...]
    pieces = _split3(g)
    ps = _dot(seg, pieces[0]) + _dot(seg, pieces[1]) + _dot(seg, pieces[2])
    lvl = lvl_ref[...]
    ones = jnp.ones((length, dv), BF16)

    def prefix(j):
        return ps[j * length:(j + 1) * length]

    def suffix(j):
        return ps[(n_lv + j) * length:(n_lv + j + 1) * length]

    for h in range(heads):
        cs = slice(h * dk, (h + 1) * dk)
        vs = slice(h * dv, (h + 1) * dv)
        qh, kh, gh, vh = q[:, cs], k[:, cs], g[:, cs], v[:, vs]
        vb = vh.astype(BF16)
        scores = jnp.zeros((length, length), F32)
        for j in range(n_lv):
            if j == 0:
                qe, ke = qh * jnp.exp(gh), kh
            else:
                qe, ke = qh * jnp.exp(prefix(j - 1)[:, cs]), kh * jnp.exp(suffix(j - 1)[:, cs])
            sc = lax.dot_general(qe.astype(BF16), ke.astype(BF16), NT_DIMS, preferred_element_type=F32)
            scores = scores + jnp.where(lvl == j, sc, 0.0)
        diag = jnp.sum(qh * kh, axis=-1, keepdims=True)
        s_old = s_scr[h]
        q_in = qh * jnp.exp(prefix(n_lv - 1)[:, cs])
        o = _dot(scores.astype(BF16), vb) + diag * vh + _dot(q_in.astype(BF16), s_old.astype(BF16))
        k_out = (kh * jnp.exp(suffix(n_lv - 1)[:, cs])).astype(BF16)
        total = sum(lax.dot_general(p[:, cs], ones, TN_DIMS, preferred_element_type=F32) for p in pieces)
        s_new = jnp.exp(total) * s_old + lax.dot_general(k_out, vb, TN_DIMS, preferred_element_type=F32)
        s_scr[h] = s_new
        slast_ref[h] = s_new
        o = o * lax.rsqrt(jnp.mean(o * o, axis=-1, keepdims=True) + EPS) * ng_ref[:, vs] * gate[:, vs]
        o_ref[:, vs] = o


def gated_linear(z, mode, prm, s0, *, length, t_valid):
    bsz, t, _ = z.shape
    heads = HG_HEADS
    dk, dv = (HG_HEAD_DIM, HG_HEAD_DIM) if mode == "hgrn" else (GLA_DK, GLA_DV)
    const = lambda b, i: (0, 0)
    col = lambda width, j: pl.BlockSpec((None, length, width), lambda b, i: (b, i, j))
    seg = jnp.asarray(_seg_matrices(length), BF16)
    lvl = jnp.asarray(_pair_level(length))
    if mode == "hgrn":
        ins = [z, z, z, z, prm["lb"], prm["norm_g"]]
        specs = [col(HALF, 1), col(HALF, 2), col(HALF, 3), col(HALF, 4),
                 pl.BlockSpec((1, HALF), const), pl.BlockSpec((1, HALF), const)]
    else:
        ins = [z, z, z, z, z, prm["w2"], prm["b2"], prm["norm_g"]]
        specs = [col(HALF // 2, 6), col(HALF // 2, 7), col(HALF, 1), col(HALF, 2), col(LANES, 18),
                 pl.BlockSpec((LANES, HALF // 2), const), pl.BlockSpec((1, HALF // 2), const),
                 pl.BlockSpec((1, HALF), const)]
    ins += [seg, lvl, s0]
    specs += [pl.BlockSpec(seg.shape, const), pl.BlockSpec(lvl.shape, const),
              pl.BlockSpec((None, heads, dk, dv), lambda b, i: (b, 0, 0, 0))]
    kern = functools.partial(_glr_kernel, mode=mode, heads=heads, dk=dk, dv=dv, length=length, t_valid=t_valid)
    return pl.pallas_call(
        kern,
        grid=(bsz, t // length),
        in_specs=specs,
        out_specs=[pl.BlockSpec((None, length, heads * dv), lambda b, i: (b, i, 0)),
                   pl.BlockSpec((None, heads, dk, dv), lambda b, i: (b, 0, 0, 0))],
        out_shape=[jax.ShapeDtypeStruct((bsz, t, heads * dv), F32),
                   jax.ShapeDtypeStruct((bsz, heads, dk, dv), F32)],
        scratch_shapes=[pltpu.VMEM((heads, dk, dv), F32)],
        compiler_params=_params("parallel", "arbitrary"),
        name="gated_linear_" + mode,
    )(*ins)


def _swa_kernel(q_ref, kc_ref, vc_ref, kp_ref, vp_ref, sink_ref, o_ref, *, lq, first_block_has_no_prev):
    hd = SWA_HEAD_DIM
    rows = SWA_REP * lq
    jp = lax.broadcasted_iota(jnp.int32, (rows, WINDOW), 0) & (lq - 1)
    if first_block_has_no_prev:
        jp = jp + jnp.where(pl.program_id(1) > 0, 0, WINDOW)
    ok_prev = lax.broadcasted_iota(jnp.int32, (rows, WINDOW), 1) > jp
    jc = lax.broadcasted_iota(jnp.int32, (rows, lq), 0) & (lq - 1)
    ok_cur = lax.broadcasted_iota(jnp.int32, (rows, lq), 1) <= jc
    head_of_row = lax.broadcasted_iota(jnp.int32, (rows, 1), 0) >> int(math.log2(lq))
    q = q_ref[...]
    scale = hd ** -0.5
    for g in range(SWA_KV_HEADS):
        ks = slice(g * hd, (g + 1) * hd)
        qs = jnp.concatenate([q[:, (g * SWA_REP + rr) * hd:(g * SWA_REP + rr + 1) * hd]
                              for rr in range(SWA_REP)], axis=0).astype(BF16)
        sp = lax.dot_general(qs, kp_ref[:, ks].astype(BF16), NT_DIMS, preferred_element_type=F32) * scale
        sc = lax.dot_general(qs, kc_ref[:, ks].astype(BF16), NT_DIMS, preferred_element_type=F32) * scale
        sp = jnp.where(ok_prev, sp, -jnp.inf)
        sc = jnp.where(ok_cur, sc, -jnp.inf)
        sink = jnp.zeros((rows, 1), F32)
        for rr in range(SWA_REP):
            hsel = g * SWA_REP + rr
            sink = jnp.where(head_of_row == rr, sink_ref[hsel:hsel + 1, 0:1], sink)
        m = jnp.maximum(jnp.maximum(jnp.max(sp, axis=-1, keepdims=True), jnp.max(sc, axis=-1, keepdims=True)), sink)
        pp = jnp.exp(sp - m)
        pc = jnp.exp(sc - m)
        den = jnp.sum(pp, axis=-1, keepdims=True) + jnp.sum(pc, axis=-1, keepdims=True) + jnp.exp(sink - m)
        o = (_dot(pp.astype(BF16), vp_ref[:, ks].astype(BF16)) + _dot(pc.astype(BF16), vc_ref[:, ks].astype(BF16))) / den
        for rr in range(SWA_REP):
            hcol = (g * SWA_REP + rr) * hd
            o_ref[:, hcol:hcol + hd] = o[rr * lq:(rr + 1) * lq]


def swa_attention(z, k_prev, v_prev, sinks, *, lq, prompt):
    bsz, t, _ = z.shape
    kvw = SWA_KV_HEADS * SWA_HEAD_DIM
    cur = lambda j: pl.BlockSpec((None, lq, kvw), lambda b, i: (b, i, j))
    if prompt:
        prev = lambda j: pl.BlockSpec((None, lq, kvw), lambda b, i: (b, jnp.maximum(i - 1, 0), j))
        kp, vp, prev_specs = z, z, [prev(16), prev(17)]
    else:
        kp, vp = k_prev, v_prev
        prev_specs = [pl.BlockSpec((None, WINDOW, kvw), lambda b, i: (b, 0, 0))] * 2
    sink_arr = jnp.broadcast_to(sinks.reshape(SWA_HEADS, 1), (SWA_HEADS, LANES))
    kern = functools.partial(_swa_kernel, lq=lq, first_block_has_no_prev=prompt)
    return pl.pallas_call(
        kern,
        grid=(bsz, t // lq),
        in_specs=[pl.BlockSpec((None, lq, HALF), lambda b, i: (b, i, 0)), cur(16), cur(17)] + prev_specs
                 + [pl.BlockSpec((SWA_HEADS, LANES), lambda b, i: (0, 0))],
        out_specs=pl.BlockSpec((None, lq, HALF), lambda b, i: (b, i, 0)),
        out_shape=jax.ShapeDtypeStruct((bsz, t, HALF), F32),
        compiler_params=_params("parallel", "parallel"),
        name="swa_prompt" if prompt else "swa_sample",
    )(z, z, z, kp, vp, sink_arr)


def _top16(s):
    key_iota = lax.broadcasted_iota(jnp.int32, s.shape, 0).astype(F32)
    row_iota = lax.broadcasted_iota(jnp.int32, (PEER_TOPK, s.shape[1]), 0)

    def body(a, carry):
        cur, rank, vals = carry
        m = jnp.max(cur, axis=0, keepdims=True)
        idx = jnp.min(jnp.where(cur == m, key_iota, float(N_KEYS)), axis=0, keepdims=True)
        hit = key_iota == idx
        rank = jnp.where(hit, jnp.asarray(a, dtype=F32), rank)
        vals = jnp.where(row_iota == a, m, vals)
        return jnp.where(hit, -jnp.inf, cur), rank, vals

    init = (s, jnp.full(s.shape, float(PEER_TOPK), F32), jnp.zeros((PEER_TOPK, s.shape[1]), F32))
    _, rank, vals = lax.fori_loop(0, PEER_TOPK, body, init)
    return rank, vals


def _top16_tie_free(s):
    row_iota = lax.broadcasted_iota(jnp.int32, (PEER_TOPK, s.shape[1]), 0)

    def body(a, carry):
        cur, rank, vals = carry
        m = jnp.max(cur, axis=0, keepdims=True)
        hit = cur == m
        rank = jnp.where(hit, jnp.asarray(a, dtype=F32), rank)
        vals = jnp.where(row_iota == a, m, vals)
        return jnp.where(hit, -jnp.inf, cur), rank, vals

    init = (s, jnp.full(s.shape, float(PEER_TOPK), F32), jnp.zeros((PEER_TOPK, s.shape[1]), F32))
    _, rank, vals = lax.fori_loop(0, PEER_TOPK, body, init)
    ranked = jnp.sum(jnp.where(rank < float(PEER_TOPK), 1.0, 0.0), axis=0, keepdims=True)
    has_tie = jnp.max(ranked) > float(PEER_TOPK)
    return lax.cond(has_tie, lambda: _top16(s), lambda: (rank, vals))


def _merge16(v0, v1):
    lanes = v0.shape[1]
    a_iota = lax.broadcasted_iota(jnp.int32, (PEER_TOPK, lanes), 0).astype(F32)
    top = v0[0:1] + v1[0:1]

    def body(_, carry):
        front, count, z = carry
        m = jnp.max(front, axis=0, keepdims=True)
        a_star = jnp.min(jnp.where(front == m, a_iota, float(PEER_TOPK)), axis=0, keepdims=True)
        hit = a_iota == a_star
        count = jnp.where(hit, count + 1.0, count)
        cn = jnp.sum(jnp.where(hit, count, 0.0), axis=0, keepdims=True)
        nxt = jnp.sum(jnp.where(a_iota == cn, v1, 0.0), axis=0, keepdims=True)
        front = jnp.where(hit, jnp.where(cn < float(PEER_TOPK), v0 + nxt, -jnp.inf), front)
        return front, count, z + jnp.exp(m - top)

    init = (v0 + v1[0:1], jnp.zeros((PEER_TOPK, lanes), F32), jnp.zeros((1, lanes), F32))
    _, count, z = lax.fori_loop(0, PEER_TOPK, body, init)
    return count, z


def _peer_topk_kernel(x_ref, g_ref, sc_ref, sh_ref, wqt_ref, keys_ref,
                      ht_ref, rank1_ref, e1_ref, cnt0_ref, e0_ref, qt_scr, s_scr):
    tt = x_ref.shape[0]
    h = _norm_mod(x_ref[...], g_ref[...], sc_ref[...], sh_ref[...])
    htb = h.T.astype(BF16)
    ht_ref[...] = htb
    qt_scr[...] = _dot(wqt_ref[...], htb)
    for hh in range(PEER_HEADS):
        for c in range(2):
            r0 = (hh * 2 + c) * PEER_QHALF
            s_scr[c] = _dot(keys_ref[c], qt_scr[r0:r0 + PEER_QHALF, :].astype(BF16))
        ranks0, vals0, vals1 = [], [], []
        for lg in range(tt // LANES):
            ls = slice(lg * LANES, (lg + 1) * LANES)
            rank0, v0 = _top16_tie_free(s_scr[0, :, ls])
            rank1, v1 = _top16_tie_free(s_scr[1, :, ls])
            rank1_ref[hh, :, ls] = rank1
            e1_ref[hh, :, ls] = jnp.exp(s_scr[1, :, ls] - v1[0:1])
            ranks0.append(rank0)
            vals0.append(v0)
            vals1.append(v1)
        v0_all = jnp.concatenate(vals0, axis=1)
        count, z = _merge16(v0_all, jnp.concatenate(vals1, axis=1))
        for lg in range(tt // LANES):
            ls = slice(lg * LANES, (lg + 1) * LANES)
            cnt0 = jnp.zeros((N_KEYS, LANES), F32)
            for a in range(PEER_TOPK):
                cnt0 = cnt0 + jnp.where(ranks0[lg] == float(a), count[a:a + 1, ls], 0.0)
            cnt0_ref[hh, :, ls] = cnt0
            e0_ref[hh, :, ls] = jnp.exp(s_scr[0, :, ls] - v0_all[0:1, ls]) / z[:, ls]


def peer_topk(x, g, sc, sh, wqt, keys, tt):
    bsz, t, _ = x.shape
    n = bsz * t
    nq = wqt.shape[0]
    tok = lambda b, i: (0, 0, b * (t // tt) + i)
    stat = pl.BlockSpec((PEER_HEADS, N_KEYS, tt), tok)
    stat_shape = jax.ShapeDtypeStruct((PEER_HEADS, N_KEYS, n), F32)
    return pl.pallas_call(
        _peer_topk_kernel,
        grid=(bsz, t // tt),
        in_specs=[pl.BlockSpec((None, tt, D_MODEL), lambda b, i: (b, i, 0)),
                  pl.BlockSpec((1, D_MODEL), lambda b, i: (0, 0)),
                  _mod_spec(sc, tt), _mod_spec(sh, tt),
                  pl.BlockSpec((nq, D_MODEL), lambda b, i: (0, 0)),
                  pl.BlockSpec((2, N_KEYS, PEER_QHALF), lambda b, i: (0, 0, 0))],
        out_specs=[pl.BlockSpec((D_MODEL, tt), lambda b, i: (0, b * (t // tt) + i)), stat, stat, stat, stat],
        out_shape=[jax.ShapeDtypeStruct((D_MODEL, n), BF16), stat_shape, stat_shape, stat_shape, stat_shape],
        scratch_shapes=[pltpu.VMEM((nq, tt), F32), pltpu.VMEM((2, N_KEYS, tt), F32)],
        compiler_params=_params("parallel", "parallel"),
        name="peer_topk",
    )(x, g.reshape(1, D_MODEL), sc, sh, wqt, keys)


def _peer_dense_kernel(ht_ref, rank1_ref, e1_ref, cnt0_ref, e0_ref, u_ref, vt_ref, x_ref, gate_ref,
                       o_ref, acc_scr, a_scr, h_scr):
    e = pl.program_id(2)
    et, tt = a_scr.shape
    nb = et // N_KEYS

    @pl.when(e == 0)
    def _():
        acc_scr[...] = jnp.zeros_like(acc_scr)

    a_scr[...] = _dot(u_ref[...], ht_ref[...])

    for ib in range(nb):
        rs = slice(ib * N_KEYS, (ib + 1) * N_KEYS)
        for lg in range(tt // LANES):
            ls = slice(lg * LANES, (lg + 1) * LANES)
            w = jnp.zeros((N_KEYS, LANES), F32)
            for hh in range(PEER_HEADS):
                cnt = cnt0_ref[hh, ib:ib + 1, ls]
                e0 = e0_ref[hh, ib:ib + 1, ls]
                w = w + jnp.where(rank1_ref[hh, :, ls] < cnt, e1_ref[hh, :, ls] * e0, 0.0)
            h_scr[rs, ls] = (_gelu(a_scr[rs, ls]) * w).astype(BF16)
    acc_scr[...] += _dot(vt_ref[...], h_scr[...])

    @pl.when(e == pl.num_programs(2) - 1)
    def _():
        o_ref[...] = x_ref[...] + gate_ref[...] * acc_scr[...].T


def peer_dense(ht, stats, u_bf, vt_bf, x, gate, tt, et):
    bsz, t, _ = x.shape
    tok = lambda b, i, e: (0, 0, b * (t // tt) + i)
    stat = pl.BlockSpec((PEER_HEADS, N_KEYS, tt), tok)
    if gate.shape[1] == 1:
        gate_spec = pl.BlockSpec((None, 1, D_MODEL), lambda b, i, e: (b, 0, 0))
    else:
        gate_spec = pl.BlockSpec((None, tt, D_MODEL), lambda b, i, e: (b, i, 0))
    return pl.pallas_call(
        _peer_dense_kernel,
        grid=(bsz, t // tt, N_EXPERTS // et),
        in_specs=[pl.BlockSpec((D_MODEL, tt), lambda b, i, e: (0, b * (t // tt) + i)),
                  stat, stat,
                  pl.BlockSpec((PEER_HEADS, et // N_KEYS, tt), lambda b, i, e: (0, e, b * (t // tt) + i)),
                  pl.BlockSpec((PEER_HEADS, et // N_KEYS, tt), lambda b, i, e: (0, e, b * (t // tt) + i)),
                  pl.BlockSpec((et, D_MODEL), lambda b, i, e: (e, 0)),
                  pl.BlockSpec((D_MODEL, et), lambda b, i, e: (0, e)),
                  pl.BlockSpec((None, tt, D_MODEL), lambda b, i, e: (b, i, 0)),
                  gate_spec],
        out_specs=pl.BlockSpec((None, tt, D_MODEL), lambda b, i, e: (b, i, 0)),
        out_shape=jax.ShapeDtypeStruct((bsz, t, D_MODEL), F32),
        scratch_shapes=[pltpu.VMEM((D_MODEL, tt), F32), pltpu.VMEM((et, tt), F32), pltpu.VMEM((et, tt), BF16)],
        compiler_params=_params("parallel", "parallel", "arbitrary"),
        name="peer_dense",
    )(ht, *stats, u_bf, vt_bf, x, gate)


def peer_layer(x, g, sc, sh, gate, w, tt, et):
    ht, *stats = peer_topk(x, g, sc, sh, w["wqt"], w["keys"], tt)
    return peer_dense(ht, stats, w["u"], w["vt"], x, gate, tt, et)


def _odd_in_weight(w_in_odd):
    qa, ka, va, qd, kd, vd, glr, og = jnp.split(w_in_odd, [512, 640, 768, 1024, 1280, 1792, 1808], axis=1)
    glr = jnp.pad(glr, ((0, 0), (0, LANES - GLA_RANK)))
    return jnp.concatenate([qa, vd, og, qd, kd, ka, va, glr], axis=1).astype(BF16)


def _prepare_weights(w):
    p = dict(w)
    p["lb0"] = jax.nn.softmax(w["hgrn_lb_logits"], axis=0)[0].reshape(1, HALF)
    p["w_in_even_bf"] = w["w_in_even"][0].astype(BF16)
    p["w_out_even_bf"] = w["w_out_even"][0].astype(BF16)
    p["w_in_odd_bf"] = _odd_in_weight(w["w_in_odd"][0])
    p["w_out_odd_bf"] = w["w_out_odd"][0].astype(BF16)
    p["s5"] = s5_params(w["s5_lambda_re"][0], w["s5_lambda_im"][0], w["s5_log_dt"][0], w["s5_b_re"][0],
                        w["s5_b_im"][0], w["s5_c_re"][0], w["s5_c_im"][0], w["s5_d"][0], w["s5_glu_w"][0],
                        w["s5_glu_b"][0], 8)
    p["hgrn"] = dict(lb=p["lb0"], norm_g=w["hgrn_norm_g"][0].reshape(1, HALF))
    p["gla"] = dict(w2=jnp.pad(w["gla_w_gate2"][0], ((0, LANES - GLA_RANK), (0, 0))).astype(BF16),
                    b2=w["gla_b_gate"][0].reshape(1, HALF // 2), norm_g=w["gla_norm_g"][0].reshape(1, HALF))
    p["peer"] = [dict(wqt=w["peer_wq"][l].T.astype(BF16), keys=w["peer_keys"][l].astype(BF16),
                      u=w["peer_u"][l].astype(BF16), vt=w["peer_v"][l].T.astype(BF16)) for l in range(DEPTH)]
    return p


def _trunk(x, mod, p, *, sample, s5_h0, hg_s0, k_buf, v_buf, gla_s0):
    bsz, t, _ = x.shape
    rows = bsz * t
    if sample:
        xr = x.reshape(1, rows, D_MODEL)
        tm = rows
        expand = lambda m: jnp.repeat(m, t, axis=0).reshape(1, rows, D_MODEL)
    else:
        xr = x
        tm = 512
        expand = lambda m: m.reshape(bsz, 1, D_MODEL)
    out = {}
    for l in range(DEPTH):
        sh1, sc1, g1, sh2, sc2, g2 = [expand(m) for m in jnp.split(mod[l], 6, axis=-1)]
        if l % 2 == 0:
            z = norm_matmul(xr, p["norm_mix_g"][l], sc1, sh1, p["w_in_even_bf"], tm)
            if sample:
                ya, hl = s5_layer(z[..., :HALF], p["s5"], s5_h0.reshape(1, bsz, 2 * S5_FLAT), stride=t, chained=False, nc=bsz)
                zp = jnp.pad(z.reshape(bsz, t, -1), ((0, 0), (0, SAMPLE_PAD - t), (0, 0)))
                yb, hs = gated_linear(zp, "hgrn", p["hgrn"], hg_s0, length=SAMPLE_PAD,t_valid=t)
                yb = yb[:, :t].reshape(1, rows, HALF)
            else:
                ya, hl = s5_layer(z[..., :HALF], p["s5"], s5_h0.reshape(bsz, 1, 2 * S5_FLAT), stride=8, chained=True, nc=32)
                yb, hs = gated_linear(z, "hgrn", p["hgrn"], hg_s0, length=128, t_valid=128)
            hl = hl.reshape(bsz, 2 * S5_FLAT)
            out["s5_re"] = hl[:, :S5_FLAT].reshape(1, bsz, S5_GROUPS, S5_STATE)
            out["s5_im"] = hl[:, S5_FLAT:].reshape(1, bsz, S5_GROUPS, S5_STATE)
            out["hgrn"] = hs[None]
            xr = out_proj(ya, yb, p["w_out_even_bf"], xr, g1, tm)
        else:
            z = norm_matmul(xr, p["norm_mix_g"][l], sc1, sh1, p["w_in_odd_bf"], tm)
            kvw = SWA_KV_HEADS * SWA_HEAD_DIM
            if sample:
                z3 = z.reshape(bsz, t, -1)
                zp = jnp.pad(z3, ((0, 0), (0, SAMPLE_PAD - t), (0, 0)))
                kb = k_buf.reshape(bsz, WINDOW, kvw)
                vb = v_buf.reshape(bsz, WINDOW, kvw)
                attn = swa_attention(zp, kb, vb, p["swa_sinks"][0], lq=SAMPLE_PAD, prompt=False)[:, :t]
                attn = attn.reshape(1, rows, HALF)
                new_k = jnp.concatenate([kb, z3[:, :, 2048:2048 + kvw]], axis=1)[:, -WINDOW:]
                new_v = jnp.concatenate([vb, z3[:, :, 2048 + kvw:2048 + 2 * kvw]], axis=1)[:, -WINDOW:]
                yb, gs = gated_linear(zp, "gla", p["gla"], gla_s0, length=SAMPLE_PAD,t_valid=t)
                yb = yb[:, :t].reshape(1, rows, HALF)
            else:
                attn = swa_attention(z, None, None, p["swa_sinks"][0], lq=WINDOW, prompt=True)
                new_k = z[:, -WINDOW:, 2048:2048 + kvw]
                new_v = z[:, -WINDOW:, 2048 + kvw:2048 + 2 * kvw]
                yb, gs = gated_linear(z, "gla", p["gla"], gla_s0, length=128, t_valid=128)
            out["swa_k"] = new_k.reshape(1, bsz, WINDOW, SWA_KV_HEADS, SWA_HEAD_DIM)
            out["swa_v"] = new_v.reshape(1, bsz, WINDOW, SWA_KV_HEADS, SWA_HEAD_DIM)
            out["gla"] = gs[None]
            xr = out_proj(attn, yb, p["w_out_odd_bf"], xr, g1, tm)
        xr = peer_layer(xr, p["norm_ffn_g"][l], sc2, sh2, g2, p["peer"][l], 512, 1024)
    y = final_norm(xr, p["final_norm_g"], tm).reshape(bsz, t, D_MODEL)
    return y, out


def kernel(x_prompt, x_sample, c_prompt, c_sample, state_s5_re, state_s5_im, state_hgrn, cache_swa_k, cache_swa_v, state_gla, ada_w, ada_b, norm_mix_g, norm_ffn_g, final_norm_g, w_in_even, w_out_even, s5_lambda_re, s5_lambda_im, s5_log_dt, s5_b_re, s5_b_im, s5_c_re, s5_c_im, s5_d, s5_glu_w, s5_glu_b, hgrn_lb_logits, hgrn_norm_g, w_in_odd, w_out_odd, swa_sinks, gla_w_gate2, gla_b_gate, gla_norm_g, peer_wq, peer_keys, peer_u, peer_v):
    w = dict(ada_w=ada_w, ada_b=ada_b, norm_mix_g=norm_mix_g, norm_ffn_g=norm_ffn_g, final_norm_g=final_norm_g,
             w_in_even=w_in_even, w_out_even=w_out_even, s5_lambda_re=s5_lambda_re, s5_lambda_im=s5_lambda_im,
             s5_log_dt=s5_log_dt, s5_b_re=s5_b_re, s5_b_im=s5_b_im, s5_c_re=s5_c_re, s5_c_im=s5_c_im, s5_d=s5_d,
             s5_glu_w=s5_glu_w, s5_glu_b=s5_glu_b, hgrn_lb_logits=hgrn_lb_logits, hgrn_norm_g=hgrn_norm_g,
             w_in_odd=w_in_odd, w_out_odd=w_out_odd, swa_sinks=swa_sinks, gla_w_gate2=gla_w_gate2,
             gla_b_gate=gla_b_gate, gla_norm_g=gla_norm_g, peer_wq=peer_wq, peer_keys=peer_keys,
             peer_u=peer_u, peer_v=peer_v)
    p = _prepare_weights(w)
    bp, bs = x_prompt.shape[0], x_sample.shape[0]
    pad = (-(bp + bs)) % 8
    c_all = jnp.concatenate([c_prompt, c_sample, jnp.zeros((pad, D_MODEL), F32)], axis=0)
    mod = ada_mod(c_all, ada_w, ada_b)
    y_p, o_p = _trunk(x_prompt, mod[:, :bp], p, sample=False,
                      s5_h0=jnp.zeros((bp, 2 * S5_FLAT), F32),
                      hg_s0=jnp.zeros((bp, HG_HEADS, HG_HEAD_DIM, HG_HEAD_DIM), F32),
                      k_buf=None, v_buf=None, gla_s0=jnp.zeros((bp, GLA_HEADS, GLA_DK, GLA_DV), F32))
    s5_h0 = jnp.concatenate([state_s5_re[0].reshape(bs, S5_FLAT), state_s5_im[0].reshape(bs, S5_FLAT)], axis=1)
    y_s, o_s = _trunk(x_sample, mod[:, bp:bp + bs], p, sample=True, s5_h0=s5_h0, hg_s0=state_hgrn[0],
                      k_buf=cache_swa_k[0], v_buf=cache_swa_v[0], gla_s0=state_gla[0])
    names = ("s5_re", "s5_im", "hgrn", "swa_k", "swa_v", "gla")
    return (y_p, y_s) + tuple(o_p[n] for n in names) + tuple(o_s[n] for n in names)
```
